```python
import math
import jax, jax.numpy as jnp
from jax import lax
import numpy as np

D_MODEL = 2048
BATCH = 16
SEQ = 2048
DEPTH = 4

CTX_LEN = 256
GRID_W = 64
N_EVEN = (DEPTH + 1) // 2
N_ODD = DEPTH // 2

MIX_WIDTH = D_MODEL
POOL_WIDTH = MIX_WIDTH // 2
POOL_GROUPS = 4
POOL_WINDOWS = (2, 4, 8, 16)
POOL_GW = POOL_WIDTH // POOL_GROUPS
GDN_HEADS = 8
GDN_DK = 128
GDN_DV = (MIX_WIDTH - POOL_WIDTH) // GDN_HEADS
GDN_QW = GDN_HEADS * GDN_DK
GDN_VW = GDN_HEADS * GDN_DV
GDN_QKV_W = 2 * GDN_QW + GDN_VW
GDN_CONV = 3
GDN_CHUNK = 64
EVEN_IN = POOL_WIDTH + GDN_QKV_W + GDN_VW + 4 * GDN_HEADS
DIFF_HEADS = 8
DIFF_HD = MIX_WIDTH // (2 * DIFF_HEADS)
DIFF_QW = DIFF_HEADS * 2 * DIFF_HD
ODD_IN = 3 * DIFF_QW
Q_BLOCK = 128
ROPE_THETA = 10000.0
D_FF = 5632
FFN_CONV = 3
ALPHA = (2 * DEPTH) ** 0.25
BETA_INIT = (8 * DEPTH) ** -0.25
EPS = 1e-6

kernel_name = "hybrid_pool_gdn_diffattn_dit"


def layer_norm(x, g, b):
    xf = x.astype(jnp.float32)
    mu = jnp.mean(xf, -1, keepdims=True)
    var = jnp.mean(jnp.square(xf - mu), -1, keepdims=True)
    return ((xf - mu) * lax.rsqrt(var + EPS) * g.astype(jnp.float32) + b.astype(jnp.float32)).astype(x.dtype)


def rms_norm(x, g):
    xf = x.astype(jnp.float32)
    return xf * lax.rsqrt(jnp.mean(xf * xf, -1, keepdims=True) + EPS) * g.astype(jnp.float32)


def l2norm(x):
    return x * lax.rsqrt(jnp.sum(x * x, -1, keepdims=True) + EPS)


def modulation(cond, w, b):
    return jnp.split(jax.nn.silu(cond) @ w + b, 6, axis=-1)


def dwconv_centred(x, w):
    k, ch = w.shape
    return lax.conv_general_dilated(x, w[:, None, :], window_strides=(1,), padding=[(k // 2, k // 2)],
                                    dimension_numbers=('NWC', 'WIO', 'NWC'), feature_group_count=ch)


def window_mean_minus_self(a, w):
    L = a.shape[1]
    af = a.astype(jnp.float32)
    cs = jnp.concatenate([jnp.zeros_like(af[:, :1]), jnp.cumsum(af, axis=1)], axis=1)
    t = jnp.arange(L)
    lo = jnp.clip(t - w // 2, 0, L)
    hi = jnp.clip(t + (w - 1 - w // 2) + 1, 0, L)
    cnt = (hi - lo).astype(jnp.float32)[None, :, None]
    return ((cs[:, hi] - cs[:, lo]) / cnt - af).astype(a.dtype)


def pool_mixer(a, pool_w, pool_scale):
    B, L, _ = a.shape
    grp = a.reshape(B, L, POOL_GROUPS, POOL_GW)
    pooled = jnp.stack([window_mean_minus_self(grp[:, :, gi], w) for gi, w in enumerate(POOL_WINDOWS)], axis=2)
    mixed = jnp.einsum('blgc,gcd->blgd', pooled, pool_w)
    return mixed.reshape(B, L, POOL_WIDTH) * pool_scale


def gated_delta_chunked(q, k, v, g, beta, s0):
    B, H, L, DK = q.shape
    DV = v.shape[-1]
    C = GDN_CHUNK
    N = L // C
    q, k, v = [t.astype(jnp.float32).reshape(B, H, N, C, -1) for t in (q, k, v)]
    g = g.astype(jnp.float32).reshape(B, H, N, C)
    beta = beta.astype(jnp.float32).reshape(B, H, N, C)
    gc = jnp.cumsum(g, axis=-1)
    tri_strict = jnp.tril(jnp.ones((C, C), bool), -1)
    tri_incl = jnp.tril(jnp.ones((C, C), bool))
    decay = jnp.exp(jnp.where(tri_incl, gc[..., :, None] - gc[..., None, :], -jnp.inf))
    kb = k * beta[..., None]
    lmat = jnp.where(tri_strict, jnp.einsum('bhnik,bhnjk->bhnij', kb, k) * decay, 0.0)
    eye = jnp.eye(C, dtype=jnp.float32)
    tmat = lax.linalg.triangular_solve(eye + lmat, jnp.broadcast_to(eye, lmat.shape), left_side=True,
                                       lower=True, unit_diagonal=True)
    u = jnp.einsum('bhnij,bhnjv->bhniv', tmat, v * beta[..., None])
    w = jnp.einsum('bhnij,bhnjk->bhnik', tmat, kb * jnp.exp(gc)[..., None])
    a_intra = jnp.einsum('bhnik,bhnjk->bhnij', q, k) * decay

    def step(S, inp):
        q_i, k_i, u_i, w_i, gc_i, a_i = inp
        v_new = u_i - jnp.einsum('bhck,bhkv->bhcv', w_i, S)
        o = (jnp.einsum('bhck,bhkv->bhcv', q_i * jnp.exp(gc_i)[..., None], S)
             + jnp.einsum('bhij,bhjv->bhiv', a_i, v_new))
        g_last = gc_i[..., -1]
        S = (S * jnp.exp(g_last)[..., None, None]
             + jnp.einsum('bhck,bhcv->bhkv', k_i * jnp.exp(g_last[..., None] - gc_i)[..., None], v_new))
        return S, o

    xs = tuple(jnp.moveaxis(t, 2, 0) for t in (q, k, u, w, gc, a_intra))
    S, o = lax.scan(step, s0.astype(jnp.float32), xs)
    return jnp.moveaxis(o, 0, 2).reshape(B, H, L, DV), S


def gdn_prepare(p, conv_w, a_log, dt_bias):
    B, L, _ = p.shape
    o = POOL_WIDTH
    qkv = jax.nn.silu(dwconv_centred(p[..., o:o + GDN_QKV_W], conv_w))
    o += GDN_QKV_W
    z = p[..., o:o + GDN_VW]
    o += GDN_VW
    beta_raw = p[..., o:o + 2 * GDN_HEADS]
    o += 2 * GDN_HEADS
    alpha_raw = p[..., o:o + 2 * GDN_HEADS]

    def heads(t, d):
        return t.reshape(B, L, GDN_HEADS, d).transpose(0, 2, 1, 3).astype(jnp.float32)

    def dirs(t):
        return t.astype(jnp.float32).reshape(B, L, 2, GDN_HEADS).transpose(2, 0, 3, 1)

    q = l2norm(heads(qkv[..., :GDN_QW], GDN_DK)) * (GDN_DK ** -0.5)
    k = l2norm(heads(qkv[..., GDN_QW:2 * GDN_QW], GDN_DK))
    v = heads(qkv[..., 2 * GDN_QW:], GDN_DV)
    beta = jax.nn.sigmoid(dirs(beta_raw))
    g = (-jnp.exp(a_log.astype(jnp.float32))[:, None, :, None]
         * jax.nn.softplus(dirs(alpha_raw) + dt_bias.astype(jnp.float32)[:, None, :, None]))
    return p[..., :POOL_WIDTH], q, k, v, z, g, beta


def flip_seq(t):
    return jnp.flip(t, axis=2)


def pool_delta_merge(a, o, z, pool_w, pool_scale, norm_g, w_out):
    B, L, _ = a.shape
    ya = pool_mixer(a, pool_w, pool_scale)
    ob = rms_norm(o.transpose(0, 2, 1, 3), norm_g) * jax.nn.silu(z.reshape(B, L, GDN_HEADS, GDN_DV).astype(jnp.float32))
    yb = ob.reshape(B, L, GDN_VW).astype(a.dtype)
    return jnp.concatenate([ya, yb], axis=-1) @ w_out


def pool_delta_mixer(ux, uh, w_in, conv_w, a_log, dt_bias, pool_w, pool_scale, norm_g, w_out, with_ctx):
    ax, qx, kx, vx, zx, gx, bx = gdn_prepare(ux @ w_in, conv_w, a_log, dt_bias)
    ah, qh, kh, vh, zh, gh, bh = gdn_prepare(uh @ w_in, conv_w, a_log, dt_bias)
    s0 = jnp.zeros((uh.shape[0], GDN_HEADS, GDN_DK, GDN_DV), jnp.float32)
    oh_f, sh_f = gated_delta_chunked(qh, kh, vh, gh[0], bh[0], s0)
    oh_b, sh_b = gated_delta_chunked(*map(flip_seq, (qh, kh, vh, gh[1], bh[1])), s0)
    ox_f, _ = gated_delta_chunked(qx, kx, vx, gx[0], bx[0], sh_f)
    ox_b, _ = gated_delta_chunked(*map(flip_seq, (qx, kx, vx, gx[1], bx[1])), sh_b)
    yx = pool_delta_merge(ax, ox_f + flip_seq(ox_b), zx, pool_w, pool_scale, norm_g, w_out)
    yh = pool_delta_merge(ah, oh_f + flip_seq(oh_b), zh, pool_w, pool_scale, norm_g, w_out) if with_ctx else None
    return yx, yh


def axial_rope_tables(rows):
    row = jnp.repeat(jnp.arange(rows), GRID_W).astype(jnp.float32)
    col = jnp.tile(jnp.arange(GRID_W), rows).astype(jnp.float32)
    n_freq = DIFF_HD // 4
    inv = ROPE_THETA ** (-jnp.arange(n_freq, dtype=jnp.float32) / n_freq)
    ang = jnp.concatenate([row[:, None] * inv, col[:, None] * inv], axis=-1)
    return jnp.cos(ang), jnp.sin(ang)


def apply_rope(t, cos, sin):
    half = t.shape[-1] // 2
    c = cos[None, :, None, None, :].astype(t.dtype)
    s = sin[None, :, None, None, :].astype(t.dtype)
    t1, t2 = t[..., :half], t[..., half:]
    return jnp.concatenate([t1 * c - t2 * s, t2 * c + t1 * s], axis=-1)


def diff_attn_core(q, k, v, lam):
    s = jnp.einsum('bqhcd,bkhcd->bhcqk', q, k, preferred_element_type=jnp.float32) * (DIFF_HD ** -0.5)
    p = jax.nn.softmax(s, axis=-1)
    a = p[:, :, 0] - lam * p[:, :, 1]
    return jnp.einsum('bhqk,bkhe->bqhe', a.astype(v.dtype), v)


def diff_attention_mixer(ux, uh, w_in, lam_q1, lam_k1, lam_q2, lam_k2, subln_g, w_out, lam_init, cos, sin, with_ctx):
    B, S, _ = ux.shape

    def qkv(u):
        p = u @ w_in
        L = u.shape[1]
        q = p[..., :DIFF_QW].reshape(B, L, DIFF_HEADS, 2, DIFF_HD)
        k = p[..., DIFF_QW:2 * DIFF_QW].reshape(B, L, DIFF_HEADS, 2, DIFF_HD)
        v = p[..., 2 * DIFF_QW:].reshape(B, L, DIFF_HEADS, 2 * DIFF_HD)
        return q, k, v

    def finish(o):
        L = o.shape[1]
        y = (rms_norm(o, subln_g) * (1.0 - lam_init)).reshape(B, L, DIFF_QW).astype(ux.dtype)
        return y @ w_out

    qx, kx, vx = qkv(ux)
    qh, kh, vh = qkv(uh)
    qx = apply_rope(qx, cos, sin)
    kx = apply_rope(kx, cos, sin)
    lam = (jnp.exp(jnp.sum(lam_q1.astype(jnp.float32) * lam_k1.astype(jnp.float32)))
           - jnp.exp(jnp.sum(lam_q2.astype(jnp.float32) * lam_k2.astype(jnp.float32))) + lam_init)
    k_all = jnp.concatenate([kh, kx], axis=1)
    v_all = jnp.concatenate([vh, vx], axis=1)
    nb = S // Q_BLOCK
    qb = qx.reshape(B, nb, Q_BLOCK, DIFF_HEADS, 2, DIFF_HD).swapaxes(0, 1)
    ox = lax.map(lambda qq: diff_attn_core(qq, k_all, v_all, lam), qb)
    ox = ox.swapaxes(0, 1).reshape(B, S, DIFF_HEADS, 2 * DIFF_HD)
    yx = finish(ox)
    yh = finish(diff_attn_core(qh, kh, vh, lam)) if with_ctx else None
    return yx, yh


def conv_glu(u, w_up, conv_w, w_down):
    hg = u @ w_up
    gate, up = hg[..., :D_FF], hg[..., D_FF:]
    return (jax.nn.silu(dwconv_centred(gate, conv_w)) * up) @ w_down


def setup_inputs(seed: int = 0) -> dict:
    key = jax.random.key(seed)
    ks = iter(jax.random.split(key, 40))
    f32 = jnp.float32
    D = D_MODEL

    def nrm(shape, s):
        return jax.random.normal(next(ks), shape, f32) * s

    x = nrm((BATCH, SEQ, D), 1.0)
    c = nrm((BATCH, D), 1.0)
    ctx = nrm((BATCH, CTX_LEN, D), 1.0)
    c_ctx = nrm((D,), 1.0)
    w_mod = nrm((DEPTH, D, 6 * D), 0.5 * D ** -0.5)
    b_mod = nrm((DEPTH, 6 * D), 0.02)
    ln1_g = 1.0 + nrm((DEPTH, D), 0.02)
    ln1_b = nrm((DEPTH, D), 0.02)
    ln2_g = 1.0 + nrm((DEPTH, D), 0.02)
    ln2_b = nrm((DEPTH, D), 0.02)
    ev_w_in = nrm((N_EVEN, D, EVEN_IN), D ** -0.5)
    ev_conv = nrm((N_EVEN, GDN_CONV, GDN_QKV_W), GDN_CONV ** -0.5)
    ev_a_log = jnp.log(jax.random.uniform(next(ks), (N_EVEN, 2, GDN_HEADS), f32, 1.0, 16.0))
    dt = jnp.exp(jax.random.uniform(next(ks), (N_EVEN, 2, GDN_HEADS), f32, math.log(1e-3), math.log(1e-1)))
    ev_dt_bias = dt + jnp.log(-jnp.expm1(-dt))
    ev_pool_w = nrm((N_EVEN, POOL_GROUPS, POOL_GW, POOL_GW), POOL_GW ** -0.5)
    ev_pool_scale = 1.0 + nrm((N_EVEN, POOL_WIDTH), 0.02)
    ev_norm = 1.0 + nrm((N_EVEN, GDN_DV), 0.02)
    ev_w_out = nrm((N_EVEN, MIX_WIDTH, D), BETA_INIT * MIX_WIDTH ** -0.5)
    od_w_in = nrm((N_ODD, D, ODD_IN), D ** -0.5)
    od_lam_q1 = nrm((N_ODD, DIFF_HD), 0.1)
    od_lam_k1 = nrm((N_ODD, DIFF_HD), 0.1)
    od_lam_q2 = nrm((N_ODD, DIFF_HD), 0.1)
    od_lam_k2 = nrm((N_ODD, DIFF_HD), 0.1)
    od_subln = 1.0 + nrm((N_ODD, 2 * DIFF_HD), 0.02)
    od_w_out = nrm((N_ODD, DIFF_QW, D), BETA_INIT * DIFF_QW ** -0.5)
    ffn_w_up = nrm((DEPTH, D, 2 * D_FF), D ** -0.5)
    ffn_conv = nrm((DEPTH, FFN_CONV, D_FF), FFN_CONV ** -0.5)
    ffn_w_down = nrm((DEPTH, D_FF, D), BETA_INIT * D_FF ** -0.5)
    return {"x": x, "c": c, "ctx": ctx, "c_ctx": c_ctx, "w_mod": w_mod, "b_mod": b_mod,
            "ln1_g": ln1_g, "ln1_b": ln1_b, "ln2_g": ln2_g, "ln2_b": ln2_b,
            "ev_w_in": ev_w_in, "ev_conv": ev_conv, "ev_a_log": ev_a_log, "ev_dt_bias": ev_dt_bias,
            "ev_pool_w": ev_pool_w, "ev_pool_scale": ev_pool_scale, "ev_norm": ev_norm, "ev_w_out": ev_w_out,
            "od_w_in": od_w_in, "od_lam_q1": od_lam_q1, "od_lam_k1": od_lam_k1, "od_lam_q2": od_lam_q2,
            "od_lam_k2": od_lam_k2, "od_subln": od_subln, "od_w_out": od_w_out,
            "ffn_w_up": ffn_w_up, "ffn_conv": ffn_conv, "ffn_w_down": ffn_w_down}


def reference(x, c, ctx, c_ctx, w_mod, b_mod, ln1_g, ln1_b, ln2_g, ln2_b,
              ev_w_in, ev_conv, ev_a_log, ev_dt_bias, ev_pool_w, ev_pool_scale, ev_norm, ev_w_out,
              od_w_in, od_lam_q1, od_lam_k1, od_lam_q2, od_lam_k2, od_subln, od_w_out,
              ffn_w_up, ffn_conv, ffn_w_down):
    rows = x.shape[1] // GRID_W
    cos, sin = axial_rope_tables(rows)
    h = ctx
    for l in range(DEPTH):
        last = l == DEPTH - 1
        sx1, cx1, gx1, sx2, cx2, gx2 = [t[:, None, :] for t in modulation(c, w_mod[l], b_mod[l])]
        sh1, ch1, gh1, sh2, ch2, gh2 = modulation(c_ctx, w_mod[l], b_mod[l])
        ux = x * (1.0 + cx1) + sx1
        uh = h * (1.0 + ch1) + sh1
        i = l // 2
        if l % 2 == 0:
            yx, yh = pool_delta_mixer(ux, uh, ev_w_in[i], ev_conv[i], ev_a_log[i], ev_dt_bias[i], ev_pool_w[i],
                                      ev_pool_scale[i], ev_norm[i], ev_w_out[i], not last)
        else:
            lam_init = 0.8 - 0.6 * math.exp(-0.3 * l)
            yx, yh = diff_attention_mixer(ux, uh, od_w_in[i], od_lam_q1[i], od_lam_k1[i], od_lam_q2[i], od_lam_k2[i],
                                          od_subln[i], od_w_out[i], lam_init, cos, sin, not last)
        x = layer_norm(ALPHA * x + gx1 * yx, ln1_g[l], ln1_b[l])
        x = layer_norm(ALPHA * x + gx2 * conv_glu(x * (1.0 + cx2) + sx2, ffn_w_up[l], ffn_conv[l], ffn_w_down[l]),
                       ln2_g[l], ln2_b[l])
        if not last:
            h = layer_norm(ALPHA * h + gh1 * yh, ln1_g[l], ln1_b[l])
            h = layer_norm(ALPHA * h + gh2 * conv_glu(h * (1.0 + ch2) + sh2, ffn_w_up[l], ffn_conv[l], ffn_w_down[l]),
                           ln2_g[l], ln2_b[l])
    return x
```

```python
import functools
import math

import jax
import jax.numpy as jnp
from jax import lax
from jax.experimental import pallas as pl
from jax.experimental.pallas import tpu as pltpu

F32 = jnp.float32
BF16 = jnp.bfloat16

GRID_W = 64
POOL_GROUPS = 4
POOL_WINDOWS = (2, 4, 8, 16)
GDN_HEADS = 8
GDN_DK = 128
GDN_DV = 128
GDN_CHUNK = 64
DIFF_HEADS = 8
DIFF_HD = 128
ROPE_THETA = 10000.0
EPS = 1e-6

V7X_VMEM_BYTES = 64 * 1024 * 1024
VMEM_LIMIT = V7X_VMEM_BYTES - 8 * 1024 * 1024
LANES = 128
HALO = 16


def _cparams(sem):
    return pltpu.CompilerParams(dimension_semantics=sem, vmem_limit_bytes=VMEM_LIMIT)


def _sigmoid(x):
    return 1.0 / (1.0 + jnp.exp(-x))


def _silu(x):
    return x * _sigmoid(x)


def _dot(a, b):
    return jnp.dot(a, b, preferred_element_type=F32)


def _dot_nt(a, b):
    return lax.dot_general(a, b, (((1,), (1,)), ((), ())), preferred_element_type=F32)


def _dot_tn(a, b):
    return lax.dot_general(a, b, (((0,), (0,)), ((), ())), preferred_element_type=F32)


def _layer_norm_rows(r, g, b):
    mu = jnp.mean(r, axis=-1, keepdims=True)
    d = r - mu
    var = jnp.mean(d * d, axis=-1, keepdims=True)
    return d * lax.rsqrt(var + EPS) * g + b


def _pick(n, cands):
    for c in cands:
        if n % c == 0:
            return c
    raise ValueError(f"no tile in {cands} divides {n}")


def _mod_kernel(c_ref, w_ref, b_ref, o_ref):
    s = _silu(c_ref[...]).astype(BF16)
    o_ref[0] = _dot(s, w_ref[0].astype(BF16)) + b_ref[0]


def _modulation(cond, w_mod, b_mod):
    depth, d, n = w_mod.shape
    r = cond.shape[0]
    tn = _pick(n, (1024, 512, 256, 128))
    return pl.pallas_call(
        _mod_kernel,
        grid=(depth, n // tn),
        in_specs=[pl.BlockSpec((r, d), lambda l, j: (0, 0)),
                  pl.BlockSpec((1, d, tn), lambda l, j: (l, 0, j)),
                  pl.BlockSpec((1, 1, tn), lambda l, j: (l, 0, j))],
        out_specs=pl.BlockSpec((1, r, tn), lambda l, j: (l, 0, j)),
        out_shape=jax.ShapeDtypeStruct((depth, r, n), F32),
        compiler_params=_cparams(("parallel", "parallel")),
        name="modulation",
    )(cond, w_mod, b_mod.reshape(depth, 1, n))


def _select_mod(mod_ref, rows_are_latent, base):
    m = mod_ref[0]
    return [jnp.where(rows_are_latent, m[base + k:base + k + 1], m[base + 3 + k:base + 4 + k]) for k in range(3)]


def _mm1_kernel(*refs, seq, tm, rope_tiles, q_tiles, q_scale):
    if rope_tiles:
        x_ref, mod_ref, w_ref, cos_ref, sin_ref, o_ref, u_scr = refs
    else:
        x_ref, mod_ref, w_ref, o_ref, u_scr = refs
    i = pl.program_id(1)
    j = pl.program_id(2)

    @pl.when(j == 0)
    def _():
        rows = i * tm + lax.broadcasted_iota(jnp.int32, (tm, 1), 0)
        shift, scale, _ = _select_mod(mod_ref, rows < seq, 0)
        u_scr[...] = (x_ref[0] * (1.0 + scale) + shift).astype(BF16)

    acc = _dot(u_scr[...], w_ref[...])
    if not rope_tiles:
        o_ref[0] = acc.astype(o_ref.dtype)
        return

    @pl.when(j < rope_tiles)
    def _():
        sc = jnp.where(j < q_tiles, q_scale, 1.0).astype(F32)
        cos = cos_ref[...] * sc
        sin = sin_ref[...] * sc
        for g in range(acc.shape[1] // LANES):
            blk = acc[:, g * LANES:(g + 1) * LANES]
            rot = pltpu.roll(blk, LANES // 2, axis=1)
            o_ref[0, :, g * LANES:(g + 1) * LANES] = (blk * cos + rot * sin).astype(o_ref.dtype)

    @pl.when(j >= rope_tiles)
    def _():
        o_ref[0] = acc.astype(o_ref.dtype)


def _mm1(xa, mod, w, seq, out_dtype, rope=None):
    b, t, d = xa.shape
    n = w.shape[1]
    tm = _pick(t, (768, 512, 256, 128))
    tn = _pick(n, (1024, 512, 256, 128))
    in_specs = [pl.BlockSpec((1, tm, d), lambda b_, i, j: (b_, i, 0)),
                pl.BlockSpec((1, 8, d), lambda b_, i, j: (b_, 0, 0)),
                pl.BlockSpec((d, tn), lambda b_, i, j: (0, j))]
    args = [xa, mod, w]
    kw = dict(seq=seq, tm=tm, rope_tiles=0, q_tiles=0, q_scale=1.0)
    if rope is not None:
        cos, sin, rope_cols, q_cols, q_scale = rope
        assert rope_cols % tn == 0 and q_cols % tn == 0
        in_specs += [pl.BlockSpec((tm, LANES), lambda b_, i, j: (i, 0)),
                     pl.BlockSpec((tm, LANES), lambda b_, i, j: (i, 0))]
        args += [cos, sin]
        kw.update(rope_tiles=rope_cols // tn, q_tiles=q_cols // tn, q_scale=q_scale)
    return pl.pallas_call(
        functools.partial(_mm1_kernel, **kw),
        grid=(b, t // tm, n // tn),
        in_specs=in_specs,
        out_specs=pl.BlockSpec((1, tm, tn), lambda b_, i, j: (b_, i, j)),
        out_shape=jax.ShapeDtypeStruct((b, t, n), out_dtype),
        scratch_shapes=[pltpu.VMEM((tm, d), BF16)],
        compiler_params=_cparams(("parallel", "parallel", "arbitrary")),
        name="mod_in_proj",
    )(*args)


def _mm2_kernel(*refs, n_in, seq, tm, alpha):
    ys = refs[:n_in]
    ws = refs[n_in:2 * n_in]
    x_ref, mod_ref, g_ref, b_ref, o_ref = refs[2 * n_in:]
    i = pl.program_id(1)
    acc = _dot(ys[0][0], ws[0][...])
    for y_ref, w_ref in zip(ys[1:], ws[1:]):
        acc = acc + _dot(y_ref[0], w_ref[...])
    rows = i * tm + lax.broadcasted_iota(jnp.int32, (tm, 1), 0)
    gate = _select_mod(mod_ref, rows < seq, 0)[2]
    r = alpha * x_ref[0] + gate * acc
    o_ref[0] = _layer_norm_rows(r, g_ref[...], b_ref[...])


def _mm2(ys, ws, xa, mod, ln_g, ln_b, seq, alpha):
    b, t, d = xa.shape
    tm = _pick(t, (384, 256, 128))
    in_specs = [pl.BlockSpec((1, tm, y.shape[2]), lambda b_, i: (b_, i, 0)) for y in ys]
    in_specs += [pl.BlockSpec(w.shape, lambda b_, i: (0, 0)) for w in ws]
    in_specs += [pl.BlockSpec((1, tm, d), lambda b_, i: (b_, i, 0)),
                 pl.BlockSpec((1, 8, d), lambda b_, i: (b_, 0, 0)),
                 pl.BlockSpec((1, d), lambda b_, i: (0, 0)),
                 pl.BlockSpec((1, d), lambda b_, i: (0, 0))]
    return pl.pallas_call(
        functools.partial(_mm2_kernel, n_in=len(ys), seq=seq, tm=tm, alpha=alpha),
        grid=(b, t // tm),
        in_specs=in_specs,
        out_specs=pl.BlockSpec((1, tm, d), lambda b_, i: (b_, i, 0)),
        out_shape=jax.ShapeDtypeStruct((b, t, d), F32),
        compiler_params=_cparams(("parallel", "parallel")),
        name="out_proj_ln",
    )(*ys, *ws, xa, mod, ln_g.reshape(1, d), ln_b.reshape(1, d))


def _ffn_kernel(x_ref, xp_ref, xn_ref, mod_ref, wg_ref, wu_ref, cw_ref, wd_ref, g_ref, b_ref, o_ref,
                u_scr, acc_scr, *, seq, t_all, tm, alpha):
    i = pl.program_id(1)
    j = pl.program_id(2)
    nj = pl.num_programs(2)
    row0 = i * tm

    @pl.when(j == 0)
    def _():
        def modulated(x, first_row, nrows):
            rows = first_row + lax.broadcasted_iota(jnp.int32, (nrows, 1), 0)
            shift, scale, _ = _select_mod(mod_ref, rows < seq, 0)
            return (x * (1.0 + scale) + shift).astype(BF16)

        u_scr[0:HALO] = modulated(xp_ref[0], row0 - HALO, HALO)
        u_scr[HALO:HALO + tm] = modulated(x_ref[0], row0, tm)
        u_scr[HALO + tm:2 * HALO + tm] = modulated(xn_ref[0], row0 + tm, HALO)
        acc_scr[...] = jnp.zeros_like(acc_scr)

    gate_ext = _dot(u_scr[...], wg_ref[...])
    up = _dot(u_scr[HALO:HALO + tm], wu_ref[...])
    gm = gate_ext[HALO:HALO + tm]
    lrow = lax.broadcasted_iota(jnp.int32, (tm, 1), 0)
    grow = row0 + lrow
    prev = jnp.where(lrow == 0, gate_ext[HALO - 1:HALO], pltpu.roll(gm, 1, axis=0))
    nxt = jnp.where(lrow == tm - 1, gate_ext[HALO + tm:HALO + tm + 1], pltpu.roll(gm, tm - 1, axis=0))
    prev = jnp.where((grow == 0) | (grow == seq), 0.0, prev)
    nxt = jnp.where((grow == seq - 1) | (grow == t_all - 1), 0.0, nxt)
    cw = cw_ref[...]
    conv = cw[0:1] * prev + cw[1:2] * gm + cw[2:3] * nxt
    h = (_silu(conv) * up).astype(BF16)
    acc_scr[...] += _dot(h, wd_ref[...])

    @pl.when(j == nj - 1)
    def _():
        gate = _select_mod(mod_ref, grow < seq, 0)[2]
        r = alpha * x_ref[0] + gate * acc_scr[...]
        o_ref[0] = _layer_norm_rows(r, g_ref[...], b_ref[...])


def _ffn(xa, mod, w_up, conv_w, w_down, ln_g, ln_b, seq, alpha):
    b, t, d = xa.shape
    dff = w_down.shape[0]
    tm = _pick(t, (768, 512, 256, 128))
    tf = _pick(dff, (512, 256, 128))
    nj = dff // tf
    hb = tm // HALO
    last_hb = t // HALO - 1
    return pl.pallas_call(
        functools.partial(_ffn_kernel, seq=seq, t_all=t, tm=tm, alpha=alpha),
        grid=(b, t // tm, nj),
        in_specs=[pl.BlockSpec((1, tm, d), lambda b_, i, j: (b_, i, 0)),
                  pl.BlockSpec((1, HALO, d), lambda b_, i, j: (b_, jnp.maximum(i * hb - 1, 0), 0)),
                  pl.BlockSpec((1, HALO, d), lambda b_, i, j: (b_, jnp.minimum((i + 1) * hb, last_hb), 0)),
                  pl.BlockSpec((1, 8, d), lambda b_, i, j: (b_, 0, 0)),
                  pl.BlockSpec((d, tf), lambda b_, i, j: (0, j)),
                  pl.BlockSpec((d, tf), lambda b_, i, j: (0, nj + j)),
                  pl.BlockSpec((3, tf), lambda b_, i, j: (0, j)),
                  pl.BlockSpec((tf, d), lambda b_, i, j: (j, 0)),
                  pl.BlockSpec((1, d), lambda b_, i, j: (0, 0)),
                  pl.BlockSpec((1, d), lambda b_, i, j: (0, 0))],
        out_specs=pl.BlockSpec((1, tm, d), lambda b_, i, j: (b_, i, 0)),
        out_shape=jax.ShapeDtypeStruct((b, t, d), F32),
        scratch_shapes=[pltpu.VMEM((tm + 2 * HALO, d), BF16), pltpu.VMEM((tm, d), F32)],
        compiler_params=_cparams(("parallel", "parallel", "arbitrary")),
        name="conv_glu_ffn",
    )(xa, xa, xa, mod, w_up, w_up, conv_w, w_down, ln_g.reshape(1, d), ln_b.reshape(1, d))


def _pool_kernel(a_ref, w_ref, s_ref, o_ref, *, seq, t_all):
    g = pl.program_id(1)
    a = a_ref[0].astype(F32)
    t = lax.broadcasted_iota(jnp.int32, (t_all, 1), 0)
    s0 = jnp.where(t < seq, 0, seq)
    s1 = jnp.where(t < seq, seq, t_all)
    for gi, win in enumerate(POOL_WINDOWS):
        @pl.when(g == gi)
        def _(win=win):
            tot = a
            for dlt in range(-(win // 2), win - win // 2):
                if dlt == 0:
                    continue
                shifted = pltpu.roll(a, (-dlt) % t_all, axis=0)
                ok = (t + dlt >= s0) & (t + dlt < s1)
                tot = tot + jnp.where(ok, shifted, 0.0)
            lo = jnp.maximum(t - win // 2, s0)
            hi = jnp.minimum(t + (win - win // 2), s1)
            cnt = (hi - lo).astype(F32)
            pooled = (tot * (1.0 / cnt) - a).astype(BF16)
            o_ref[0] = (_dot(pooled, w_ref[0]) * s_ref[0]).astype(o_ref.dtype)


def _pool_mixer(p, pool_w, pool_scale, seq):
    b, t, _ = p.shape
    g, gw, _ = pool_w.shape
    return pl.pallas_call(
        functools.partial(_pool_kernel, seq=seq, t_all=t),
        grid=(b, g),
        in_specs=[pl.BlockSpec((1, t, gw), lambda b_, g_: (b_, 0, g_)),
                  pl.BlockSpec((1, gw, gw), lambda b_, g_: (g_, 0, 0)),
                  pl.BlockSpec((1, 1, gw), lambda b_, g_: (g_, 0, 0))],
        out_specs=pl.BlockSpec((1, t, gw), lambda b_, g_: (b_, 0, g_)),
        out_shape=jax.ShapeDtypeStruct((b, t, g * gw), BF16),
        compiler_params=_cparams(("parallel", "parallel")),
        name="pool_mixer",
    )(p, pool_w, pool_scale.reshape(g, 1, gw))


def _gate_values(raw, a_log, dt, is_beta):
    x = raw + dt
    softplus = jnp.maximum(x, 0.0) + jnp.log(1.0 + jnp.exp(-jnp.abs(x)))
    return jnp.where(is_beta, _sigmoid(raw), -jnp.exp(a_log) * softplus)


def _gdn_kernel(q_ref, k_ref, v_ref, z_ref, gc_ref, gr_ref, pc_ref, pr_ref, cwq_ref, cwk_ref, cwv_ref, ng_ref,
                o_ref, u_scr, wq_scr, kd_scr, a_scr, eg_scr, of_scr, ob_scr, *, seq, t_all):
    c = GDN_CHUNK
    nc = t_all // c
    seq_c = seq // c
    ctx_c = nc - seq_c
    ri = lax.broadcasted_iota(jnp.int32, (c, c), 0)
    ci = lax.broadcasted_iota(jnp.int32, (c, c), 1)
    lower = ci <= ri
    upper = ci >= ri
    eye = (ci == ri).astype(F32)
    row = lax.broadcasted_iota(jnp.int32, (c, 1), 0)
    lane4 = lax.broadcasted_iota(jnp.int32, (c, 4), 1)
    sub4 = lax.broadcasted_iota(jnp.int32, (4, c), 0)
    pc = pc_ref[0]
    pr = pr_ref[0]

    def precompute(n, carry):
        r0 = pl.multiple_of(n * c, c)
        has_prev = jnp.where((n != 0) & (n != seq_c), 1.0, 0.0).astype(F32)
        has_next = jnp.where((n != seq_c - 1) & (n != nc - 1), 1.0, 0.0).astype(F32)
        pidx = jnp.maximum(r0 - 1, 0)
        nidx = jnp.minimum(r0 + c, t_all - 1)

        def conv_silu(ref, cw_ref):
            xm = ref[0, pl.ds(r0, c), :]
            xp = jnp.where(row == 0, ref[0, pl.ds(pidx, 1), :] * has_prev, pltpu.roll(xm, 1, axis=0))
            xn = jnp.where(row == c - 1, ref[0, pl.ds(nidx, 1), :] * has_next, pltpu.roll(xm, c - 1, axis=0))
            cw = cw_ref[...]
            return _silu(cw[0:1] * xp + cw[1:2] * xm + cw[2:3] * xn)

        def l2n(x):
            return x * lax.rsqrt(jnp.sum(x * x, axis=-1, keepdims=True) + EPS)

        q = l2n(conv_silu(q_ref, cwq_ref)) * (GDN_DK ** -0.5)
        k = l2n(conv_silu(k_ref, cwk_ref))
        v = conv_silu(v_ref, cwv_ref)
        qb = q.astype(BF16)
        kb16 = k.astype(BF16)
        qk = _dot_nt(qb, kb16)
        kk = _dot_nt(kb16, kb16)
        gcol = _gate_values(gc_ref[0, 0, pl.ds(r0, c), :], pc[0:1], pc[1:2], lane4 < 2)
        grow = _gate_values(gr_ref[0, 0, n], pr[:, 0:1], pr[:, 1:2], sub4 < 2)
        for d in range(2):
            beta = gcol[:, d:d + 1]
            g_c = gcol[:, 2 + d:3 + d]
            g_r = grow[2 + d:3 + d, :]
            incl = lower if d == 0 else upper
            strict = (ci < ri) if d == 0 else (ci > ri)
            inclt = upper if d == 0 else lower
            gcc = jnp.sum(jnp.where(incl, g_r, 0.0), axis=1, keepdims=True)
            gcr = jnp.sum(jnp.where(inclt, g_c, 0.0), axis=0, keepdims=True)
            decay = jnp.exp(jnp.where(incl, gcc - gcr, -jnp.inf))
            g_last = gcc[c - 1:c] if d == 0 else gcc[0:1]
            lmat = jnp.where(strict, kk * beta * decay, 0.0)
            tmat = eye - lmat
            pw = lmat
            for _ in range(int(math.log2(c)) - 1):
                pwb = pw.astype(BF16)
                pw = _dot(pwb, pwb)
                tmat = tmat + _dot(tmat.astype(BF16), pw.astype(BF16))
            egc = jnp.exp(gcc)
            rhs = jnp.concatenate([v * beta, k * (beta * egc)], axis=1).astype(BF16)
            uw = _dot(tmat.astype(BF16), rhs)
            amat = jnp.where(incl, qk * decay, 0.0)
            u_scr[d, pl.ds(r0, c), :] = uw[:, :GDN_DV]
            wq_scr[d, n] = jnp.concatenate([uw[:, GDN_DV:], q * egc], axis=0).astype(BF16)
            kd_scr[d, pl.ds(r0, c), :] = (k * jnp.exp(g_last - gcc)).astype(BF16)
            a_scr[d, n] = amat.astype(BF16)
            eg_scr[d, n] = jnp.broadcast_to(jnp.exp(g_last), (8, LANES))
        return carry

    lax.fori_loop(0, nc, precompute, 0)

    def recur(s, states):
        order = (jnp.where(s < ctx_c, seq_c + s, s - ctx_c),
                 nc - 1 - s)
        new_states = []
        for d, (st, o_scr) in enumerate(zip(states, (of_scr, ob_scr))):
            n = order[d]
            r0 = pl.multiple_of(n * c, c)
            wqs = _dot(wq_scr[d, n], st.astype(BF16))
            v_new = u_scr[d, pl.ds(r0, c), :] - wqs[:c]
            vb = v_new.astype(BF16)
            o_scr[pl.ds(r0, c), :] = wqs[c:] + _dot(a_scr[d, n], vb)
            new_states.append(st * eg_scr[d, n][0:1] + _dot_tn(kd_scr[d, pl.ds(r0, c), :], vb))
        return tuple(new_states)

    zero = jnp.zeros((GDN_DK, GDN_DV), F32)
    lax.fori_loop(0, nc, recur, (zero, zero))

    def finish(n, carry):
        r0 = pl.multiple_of(n * c, c)
        o = of_scr[pl.ds(r0, c), :] + ob_scr[pl.ds(r0, c), :]
        o = o * lax.rsqrt(jnp.mean(o * o, axis=-1, keepdims=True) + EPS) * ng_ref[...]
        o_ref[0, pl.ds(r0, c), :] = (o * _silu(z_ref[0, pl.ds(r0, c), :])).astype(o_ref.dtype)
        return carry

    lax.fori_loop(0, nc, finish, 0)


def _gdn_mixer(p, gates, conv_w, a_log, dt_bias, norm_g, seq):
    b, t, _ = p.shape
    h, c = GDN_HEADS, GDN_CHUNK
    nc = t // c
    pool_blocks = (p.shape[2] - 4 * h * GDN_DK) // LANES
    gates = gates[..., :4 * h].reshape(b, t, 4, h)
    g_col = gates.transpose(0, 3, 1, 2)
    g_row = gates.reshape(b, nc, c, 4, h).transpose(0, 4, 1, 3, 2)
    zeros = jnp.zeros((2, h), F32)
    prm = jnp.concatenate([zeros, a_log, zeros, dt_bias], axis=0).reshape(2, 4, h).astype(F32)
    p_col = prm.transpose(2, 0, 1)
    p_row = prm.transpose(2, 1, 0)
    col = lambda off: pl.BlockSpec((1, t, LANES), lambda b_, h_: (b_, 0, pool_blocks + off * h + h_))
    cw = lambda off: pl.BlockSpec((3, LANES), lambda b_, h_: (0, off * h + h_))
    return pl.pallas_call(
        functools.partial(_gdn_kernel, seq=seq, t_all=t),
        grid=(b, h),
        in_specs=[col(0), col(1), col(2), col(3),
                  pl.BlockSpec((1, 1, t, 4), lambda b_, h_: (b_, h_, 0, 0)),
                  pl.BlockSpec((1, 1, nc, 4, c), lambda b_, h_: (b_, h_, 0, 0, 0)),
                  pl.BlockSpec((1, 2, 4), lambda b_, h_: (h_, 0, 0)),
                  pl.BlockSpec((1, 4, 2), lambda b_, h_: (h_, 0, 0)),
                  cw(0), cw(1), cw(2),
                  pl.BlockSpec((1, GDN_DV), lambda b_, h_: (0, 0))],
        out_specs=pl.BlockSpec((1, t, GDN_DV), lambda b_, h_: (b_, 0, h_)),
        out_shape=jax.ShapeDtypeStruct((b, t, h * GDN_DV), BF16),
        scratch_shapes=[pltpu.VMEM((2, t, GDN_DV), F32),
                        pltpu.VMEM((2, nc, 2 * c, GDN_DK), BF16),
                        pltpu.VMEM((2, t, GDN_DK), BF16),
                        pltpu.VMEM((2, nc, c, c), BF16),
                        pltpu.VMEM((2, nc, 8, LANES), F32),
                        pltpu.VMEM((t, GDN_DV), F32),
                        pltpu.VMEM((t, GDN_DV), F32)],
        compiler_params=_cparams(("parallel", "parallel")),
        name="gated_delta",
    )(p, p, p, p, g_col, g_row, p_col, p_row, conv_w, conv_w, conv_w, norm_g.reshape(1, GDN_DV))


def _attn_kernel(q_ref, k_ref, v_ref, lam_ref, g_ref, o_ref, *, seq, t_all, tq, lam_init, q_tile0):
    i = pl.program_id(2) + q_tile0
    hd = DIFF_HD
    lp = lam_ref[...]
    lam = (jnp.exp(jnp.sum(lp[0:1] * lp[1:2], axis=-1, keepdims=True))
           - jnp.exp(jnp.sum(lp[2:3] * lp[3:4], axis=-1, keepdims=True)) + lam_init)
    q = q_ref[0]

    def attend(k, v):
        probs = []
        for cpt in range(2):
            s = _dot_nt(q[:, cpt * hd:(cpt + 1) * hd], k[:, cpt * hd:(cpt + 1) * hd])
            e = jnp.exp(s - jnp.max(s, axis=-1, keepdims=True))
            probs.append(e * (1.0 / jnp.sum(e, axis=-1, keepdims=True)))
        a = (probs[0] - lam * probs[1]).astype(BF16)
        o = _dot(a, v)
        o = o * lax.rsqrt(jnp.mean(o * o, axis=-1, keepdims=True) + EPS) * g_ref[...] * (1.0 - lam_init)
        o_ref[0] = o.astype(o_ref.dtype)

    @pl.when(i * tq < seq)
    def _():
        attend(k_ref[0], v_ref[0])

    if q_tile0 == 0 and t_all > seq:
        @pl.when(i * tq >= seq)
        def _():
            attend(k_ref[0, seq:t_all, :], v_ref[0, seq:t_all, :])


def _diff_attention(qkv, lam_p, subln_g, seq, lam_init, with_ctx):
    b, t, _ = qkv.shape
    h, w = DIFF_HEADS, 2 * DIFF_HD
    tq = 256
    assert seq % tq == 0 and t % tq == 0
    nq = (t if with_ctx else seq) // tq
    return pl.pallas_call(
        functools.partial(_attn_kernel, seq=seq, t_all=t, tq=tq, lam_init=lam_init, q_tile0=0),
        grid=(b, h, nq),
        in_specs=[pl.BlockSpec((1, tq, w), lambda b_, h_, i: (b_, i, h_)),
                  pl.BlockSpec((1, t, w), lambda b_, h_, i: (b_, 0, h + h_)),
                  pl.BlockSpec((1, t, w), lambda b_, h_, i: (b_, 0, 2 * h + h_)),
                  pl.BlockSpec((4, DIFF_HD), lambda b_, h_, i: (0, 0)),
                  pl.BlockSpec((1, w), lambda b_, h_, i: (0, 0))],
        out_specs=pl.BlockSpec((1, tq, w), lambda b_, h_, i: (b_, i, h_)),
        out_shape=jax.ShapeDtypeStruct((b, t, h * w), BF16),
        compiler_params=_cparams(("parallel", "parallel", "arbitrary")),
        name="diff_attention",
    )(qkv, qkv, qkv, lam_p, subln_g.reshape(1, w))


def _rope_tables(seq, t_all):
    rows = seq // GRID_W
    row = jnp.repeat(jnp.arange(rows), GRID_W).astype(F32)
    col = jnp.tile(jnp.arange(GRID_W), rows).astype(F32)
    n_freq = DIFF_HD // 4
    inv = ROPE_THETA ** (-jnp.arange(n_freq, dtype=F32) / n_freq)
    ang = jnp.concatenate([row[:, None] * inv, col[:, None] * inv], axis=-1)
    cos = jnp.concatenate([jnp.cos(ang), jnp.cos(ang)], axis=-1)
    sin = jnp.concatenate([-jnp.sin(ang), jnp.sin(ang)], axis=-1)
    pad = t_all - seq
    cos = jnp.concatenate([cos, jnp.ones((pad, DIFF_HD), F32)], axis=0)
    sin = jnp.concatenate([sin, jnp.zeros((pad, DIFF_HD), F32)], axis=0)
    return cos, sin


def kernel(x, c, ctx, c_ctx, w_mod, b_mod, ln1_g, ln1_b, ln2_g, ln2_b, ev_w_in, ev_conv, ev_a_log, ev_dt_bias, ev_pool_w, ev_pool_scale, ev_norm, ev_w_out, od_w_in, od_lam_q1, od_lam_k1, od_lam_q2, od_lam_k2, od_subln, od_w_out, ffn_w_up, ffn_conv, ffn_w_down):
    b, seq, d = x.shape
    t_all = seq + ctx.shape[1]
    depth = w_mod.shape[0]
    alpha = (2 * depth) ** 0.25
    pool_width = ev_pool_w.shape[1] * ev_pool_w.shape[2]
    gdn_main = pool_width + 4 * GDN_HEADS * GDN_DK
    diff_qw = DIFF_HEADS * 2 * DIFF_HD

    r_pad = -(-(b + 1) // 8) * 8
    cond = jnp.concatenate([c, c_ctx[None, :], jnp.zeros((r_pad - b - 1, d), F32)], axis=0)
    mods = _modulation(cond, w_mod, b_mod).reshape(depth, r_pad, 6, d)
    lat = mods[:, :b]
    cx = jnp.broadcast_to(mods[:, b:b + 1], lat.shape)
    zeros2 = jnp.zeros((depth, b, 2, d), F32)
    mod1 = jnp.concatenate([lat[:, :, 0:3], cx[:, :, 0:3], zeros2], axis=2)
    mod2 = jnp.concatenate([lat[:, :, 3:6], cx[:, :, 3:6], zeros2], axis=2)

    cos, sin = _rope_tables(seq, t_all)
    xa = jnp.concatenate([x, ctx], axis=1)

    for l in range(depth):
        i = l // 2
        if l % 2 == 0:
            w_in = ev_w_in[i].astype(BF16)
            w_gate = jnp.pad(w_in[:, gdn_main:], ((0, 0), (0, LANES - (w_in.shape[1] - gdn_main))))
            p = _mm1(xa, mod1[l], w_in[:, :gdn_main], seq, F32)
            gates = _mm1(xa, mod1[l], w_gate, seq, F32)
            ya = _pool_mixer(p, ev_pool_w[i].astype(BF16), ev_pool_scale[i], seq)
            yb = _gdn_mixer(p, gates, ev_conv[i], ev_a_log[i], ev_dt_bias[i], ev_norm[i], seq)
            w_out = ev_w_out[i].astype(BF16)
            xa = _mm2([ya, yb], [w_out[:pool_width], w_out[pool_width:]], xa, mod1[l], ln1_g[l], ln1_b[l], seq, alpha)
        else:
            lam_init = 0.8 - 0.6 * math.exp(-0.3 * l)
            qkv = _mm1(xa, mod1[l], od_w_in[i].astype(BF16), seq, BF16,
                       rope=(cos, sin, 2 * diff_qw, diff_qw, DIFF_HD ** -0.5))
            lam_p = jnp.stack([od_lam_q1[i], od_lam_k1[i], od_lam_q2[i], od_lam_k2[i]]).astype(F32)
            y = _diff_attention(qkv, lam_p, od_subln[i], seq, lam_init, True)
            xa = _mm2([y], [od_w_out[i].astype(BF16)], xa, mod1[l], ln1_g[l], ln1_b[l], seq, alpha)
        xa = _ffn(xa, mod2[l], ffn_w_up[l].astype(BF16), ffn_conv[l], ffn_w_down[l].astype(BF16),
                  ln2_g[l], ln2_b[l], seq, alpha)
    return xa[:, :seq]
```

```python
import functools
import math

import jax
import jax.numpy as jnp
from jax import lax
from jax.experimental import pallas as pl
from jax.experimental.pallas import tpu as pltpu

F32 = jnp.float32
BF16 = jnp.bfloat16

GRID_W = 64
POOL_GROUPS = 4
POOL_WINDOWS = (2, 4, 8, 16)
GDN_HEADS = 8
GDN_DK = 128
GDN_DV = 128
GDN_CHUNK = 64
DIFF_HEADS = 8
DIFF_HD = 128
ROPE_THETA = 10000.0
EPS = 1e-6

V7X_VMEM_BYTES = 64 * 1024 * 1024
VMEM_LIMIT = V7X_VMEM_BYTES - 8 * 1024 * 1024
LANES = 128
HALO = 16


def _cparams(sem):
    return pltpu.CompilerParams(dimension_semantics=sem, vmem_limit_bytes=VMEM_LIMIT)


def _sigmoid(x):
    return 1.0 / (1.0 + jnp.exp(-x))


def _silu(x):
    return x * _sigmoid(x)


def _dot(a, b):
    return jnp.dot(a, b, preferred_element_type=F32)


def _dot_nt(a, b):
    return lax.dot_general(a, b, (((1,), (1,)), ((), ())), preferred_element_type=F32)


def _dot_tn(a, b):
    return lax.dot_general(a, b, (((0,), (0,)), ((), ())), preferred_element_type=F32)


def _layer_norm_rows(r, g, b):
    mu = jnp.mean(r, axis=-1, keepdims=True)
    d = r - mu
    var = jnp.mean(d * d, axis=-1, keepdims=True)
    return d * lax.rsqrt(var + EPS) * g + b


def _pick(n, cands):
    for c in cands:
        if n % c == 0:
            return c
    raise ValueError(f"no tile in {cands} divides {n}")


def _mod_kernel(c_ref, w_ref, b_ref, o_ref):
    s = _silu(c_ref[...]).astype(BF16)
    o_ref[0] = _dot(s, w_ref[0].astype(BF16)) + b_ref[0]


def _modulation(cond, w_mod, b_mod):
    depth, d, n = w_mod.shape
    r = cond.shape[0]
    tn = _pick(n, (1024, 512, 256, 128))
    return pl.pallas_call(
        _mod_kernel,
        grid=(depth, n // tn),
        in_specs=[pl.BlockSpec((r, d), lambda l, j: (0, 0)),
                  pl.BlockSpec((1, d, tn), lambda l, j: (l, 0, j)),
                  pl.BlockSpec((1, 1, tn), lambda l, j: (l, 0, j))],
        out_specs=pl.BlockSpec((1, r, tn), lambda l, j: (l, 0, j)),
        out_shape=jax.ShapeDtypeStruct((depth, r, n), F32),
        compiler_params=_cparams(("parallel", "parallel")),
        name="modulation",
    )(cond, w_mod, b_mod.reshape(depth, 1, n))


def _select_mod(mod_ref, rows_are_latent, base):
    m = mod_ref[0]
    return [jnp.where(rows_are_latent, m[base + k:base + k + 1], m[base + 3 + k:base + 4 + k]) for k in range(3)]


def _mm1_kernel(*refs, seq, tm, rope_tiles, q_tiles, q_scale):
    if rope_tiles:
        x_ref, mod_ref, w_ref, cos_ref, sin_ref, o_ref, u_scr = refs
    else:
        x_ref, mod_ref, w_ref, o_ref, u_scr = refs
    i = pl.program_id(1)
    j = pl.program_id(2)

    @pl.when(j == 0)
    def _():
        rows = i * tm + lax.broadcasted_iota(jnp.int32, (tm, 1), 0)
        shift, scale, _ = _select_mod(mod_ref, rows < seq, 0)
        u_scr[...] = (x_ref[0] * (1.0 + scale) + shift).astype(BF16)

    acc = _dot(u_scr[...], w_ref[...])
    if not rope_tiles:
        o_ref[0] = acc.astype(o_ref.dtype)
        return

    @pl.when(j < rope_tiles)
    def _():
        sc = jnp.where(j < q_tiles, q_scale, 1.0).astype(F32)
        cos = cos_ref[...] * sc
        sin = sin_ref[...] * sc
        for g in range(acc.shape[1] // LANES):
            blk = acc[:, g * LANES:(g + 1) * LANES]
            rot = pltpu.roll(blk, LANES // 2, axis=1)
            o_ref[0, :, g * LANES:(g + 1) * LANES] = (blk * cos + rot * sin).astype(o_ref.dtype)

    @pl.when(j >= rope_tiles)
    def _():
        o_ref[0] = acc.astype(o_ref.dtype)


def _mm1(xa, mod, w, seq, out_dtype, rope=None):
    b, t, d = xa.shape
    n = w.shape[1]
    tm = _pick(t, (768, 512, 256, 128))
    tn = _pick(n, (1024, 512, 256, 128))
    in_specs = [pl.BlockSpec((1, tm, d), lambda b_, i, j: (b_, i, 0)),
                pl.BlockSpec((1, 8, d), lambda b_, i, j: (b_, 0, 0)),
                pl.BlockSpec((d, tn), lambda b_, i, j: (0, j))]
    args = [xa, mod, w]
    kw = dict(seq=seq, tm=tm, rope_tiles=0, q_tiles=0, q_scale=1.0)
    if rope is not None:
        cos, sin, rope_cols, q_cols, q_scale = rope
        assert rope_cols % tn == 0 and q_cols % tn == 0
        in_specs += [pl.BlockSpec((tm, LANES), lambda b_, i, j: (i, 0)),
                     pl.BlockSpec((tm, LANES), lambda b_, i, j: (i, 0))]
        args += [cos, sin]
        kw.update(rope_tiles=rope_cols // tn, q_tiles=q_cols // tn, q_scale=q_scale)
    return pl.pallas_call(
        functools.partial(_mm1_kernel, **kw),
        grid=(b, t // tm, n // tn),
        in_specs=in_specs,
        out_specs=pl.BlockSpec((1, tm, tn), lambda b_, i, j: (b_, i, j)),
        out_shape=jax.ShapeDtypeStruct((b, t, n), out_dtype),
        scratch_shapes=[pltpu.VMEM((tm, d), BF16)],
        compiler_params=_cparams(("parallel", "parallel", "arbitrary")),
        name="mod_in_proj",
    )(*args)


def _mm2_kernel(*refs, gdn, seq, tm, alpha):
    if gdn:
        ya_ref, of_ref, ob_ref, z_ref, ng_ref, wa_ref, wb_ref, x_ref, mod_ref, g_ref, b_ref, o_ref, yb_scr = refs
        for h in range(GDN_HEADS):
            sl = slice(h * GDN_DV, (h + 1) * GDN_DV)
            o = of_ref[0, :, sl] + ob_ref[0, :, sl]
            o = o * lax.rsqrt(jnp.mean(o * o, axis=-1, keepdims=True) + EPS) * ng_ref[...]
            yb_scr[:, sl] = (o * _silu(z_ref[0, :, sl])).astype(BF16)
        acc = _dot(ya_ref[0], wa_ref[...]) + _dot(yb_scr[...], wb_ref[...])
    else:
        y_ref, w_ref, x_ref, mod_ref, g_ref, b_ref, o_ref = refs
        acc = _dot(y_ref[0], w_ref[...])
    i = pl.program_id(1)
    rows = i * tm + lax.broadcasted_iota(jnp.int32, (tm, 1), 0)
    gate = _select_mod(mod_ref, rows < seq, 0)[2]
    r = alpha * x_ref[0] + gate * acc
    o_ref[0] = _layer_norm_rows(r, g_ref[...], b_ref[...])


def _mm2(ys, ws, xa, mod, ln_g, ln_b, seq, alpha, gdn=None):
    b, t, d = xa.shape
    tm = _pick(t, (384, 256, 128))
    row_blk = lambda width, cb=0: pl.BlockSpec((1, tm, width), lambda b_, i: (b_, i, cb))
    full = lambda a: pl.BlockSpec(a.shape, lambda b_, i: (0,) * a.ndim)
    in_specs = [row_blk(ys[0].shape[2])]
    args = [ys[0]]
    scratch = []
    if gdn is not None:
        o_f, o_b, p, z_blk, norm_g = gdn
        zw = o_f.shape[2]
        in_specs += [row_blk(zw), row_blk(zw), row_blk(zw, z_blk), pl.BlockSpec((1, GDN_DV), lambda b_, i: (0, 0))]
        args += [o_f, o_b, p, norm_g.reshape(1, GDN_DV)]
        scratch = [pltpu.VMEM((tm, zw), BF16)]
    in_specs += [full(w) for w in ws]
    in_specs += [row_blk(d), pl.BlockSpec((1, 8, d), lambda b_, i: (b_, 0, 0)),
                 pl.BlockSpec((1, d), lambda b_, i: (0, 0)), pl.BlockSpec((1, d), lambda b_, i: (0, 0))]
    return pl.pallas_call(
        functools.partial(_mm2_kernel, gdn=gdn is not None, seq=seq, tm=tm, alpha=alpha),
        grid=(b, t // tm),
        in_specs=in_specs,
        out_specs=row_blk(d),
        out_shape=jax.ShapeDtypeStruct((b, t, d), F32),
        scratch_shapes=scratch,
        compiler_params=_cparams(("parallel", "parallel")),
        name="out_proj_ln",
    )(*args, *ws, xa, mod, ln_g.reshape(1, d), ln_b.reshape(1, d))


def _ffn_kernel(x_ref, xp_ref, xn_ref, mod_ref, wg_ref, wu_ref, cw_ref, wd_ref, g_ref, b_ref, o_ref,
                u_scr, acc_scr, *, seq, t_all, tm, alpha):
    i = pl.program_id(1)
    j = pl.program_id(2)
    nj = pl.num_programs(2)
    row0 = i * tm

    @pl.when(j == 0)
    def _():
        def modulated(x, first_row, nrows):
            rows = first_row + lax.broadcasted_iota(jnp.int32, (nrows, 1), 0)
            shift, scale, _ = _select_mod(mod_ref, rows < seq, 0)
            return (x * (1.0 + scale) + shift).astype(BF16)

        u_scr[0:HALO] = modulated(xp_ref[0], row0 - HALO, HALO)
        u_scr[HALO:HALO + tm] = modulated(x_ref[0], row0, tm)
        u_scr[HALO + tm:2 * HALO + tm] = modulated(xn_ref[0], row0 + tm, HALO)
        acc_scr[...] = jnp.zeros_like(acc_scr)

    gate_ext = _dot(u_scr[...], wg_ref[...])
    up = _dot(u_scr[HALO:HALO + tm], wu_ref[...])
    gm = gate_ext[HALO:HALO + tm]
    lrow = lax.broadcasted_iota(jnp.int32, (tm, 1), 0)
    grow = row0 + lrow
    prev = jnp.where(lrow == 0, gate_ext[HALO - 1:HALO], pltpu.roll(gm, 1, axis=0))
    nxt = jnp.where(lrow == tm - 1, gate_ext[HALO + tm:HALO + tm + 1], pltpu.roll(gm, tm - 1, axis=0))
    prev = jnp.where((grow == 0) | (grow == seq), 0.0, prev)
    nxt = jnp.where((grow == seq - 1) | (grow == t_all - 1), 0.0, nxt)
    cw = cw_ref[...]
    conv = cw[0:1] * prev + cw[1:2] * gm + cw[2:3] * nxt
    h = (_silu(conv) * up).astype(BF16)
    acc_scr[...] += _dot(h, wd_ref[...])

    @pl.when(j == nj - 1)
    def _():
        gate = _select_mod(mod_ref, grow < seq, 0)[2]
        r = alpha * x_ref[0] + gate * acc_scr[...]
        o_ref[0] = _layer_norm_rows(r, g_ref[...], b_ref[...])


def _ffn(xa, mod, w_up, conv_w, w_down, ln_g, ln_b, seq, alpha, out_rows):
    b, t, d = xa.shape
    dff = w_down.shape[0]
    tm = _pick(t, (768, 512, 256, 128))
    tf = _pick(dff, (512, 256, 128))
    nj = dff // tf
    hb = tm // HALO
    last_hb = t // HALO - 1
    return pl.pallas_call(
        functools.partial(_ffn_kernel, seq=seq, t_all=t, tm=tm, alpha=alpha),
        grid=(b, pl.cdiv(out_rows, tm), nj),
        in_specs=[pl.BlockSpec((1, tm, d), lambda b_, i, j: (b_, i, 0)),
                  pl.BlockSpec((1, HALO, d), lambda b_, i, j: (b_, jnp.maximum(i * hb - 1, 0), 0)),
                  pl.BlockSpec((1, HALO, d), lambda b_, i, j: (b_, jnp.minimum((i + 1) * hb, last_hb), 0)),
                  pl.BlockSpec((1, 8, d), lambda b_, i, j: (b_, 0, 0)),
                  pl.BlockSpec((d, tf), lambda b_, i, j: (0, j)),
                  pl.BlockSpec((d, tf), lambda b_, i, j: (0, nj + j)),
                  pl.BlockSpec((3, tf), lambda b_, i, j: (0, j)),
                  pl.BlockSpec((tf, d), lambda b_, i, j: (j, 0)),
                  pl.BlockSpec((1, d), lambda b_, i, j: (0, 0)),
                  pl.BlockSpec((1, d), lambda b_, i, j: (0, 0))],
        out_specs=pl.BlockSpec((1, tm, d), lambda b_, i, j: (b_, i, 0)),
        out_shape=jax.ShapeDtypeStruct((b, out_rows, d), F32),
        scratch_shapes=[pltpu.VMEM((tm + 2 * HALO, d), BF16), pltpu.VMEM((tm, d), F32)],
        compiler_params=_cparams(("parallel", "parallel", "arbitrary")),
        name="conv_glu_ffn",
    )(xa, xa, xa, mod, w_up, w_up, conv_w, w_down, ln_g.reshape(1, d), ln_b.reshape(1, d))


def _pool_kernel(a_ref, w_ref, s_ref, o_ref, *, seq, t_all):
    g = pl.program_id(1)
    a = a_ref[0].astype(F32)
    t = lax.broadcasted_iota(jnp.int32, (t_all, 1), 0)
    s0 = jnp.where(t < seq, 0, seq)
    s1 = jnp.where(t < seq, seq, t_all)
    for gi, win in enumerate(POOL_WINDOWS):
        @pl.when(g == gi)
        def _(win=win):
            tot = a
            for dlt in range(-(win // 2), win - win // 2):
                if dlt == 0:
                    continue
                shifted = pltpu.roll(a, (-dlt) % t_all, axis=0)
                ok = (t + dlt >= s0) & (t + dlt < s1)
                tot = tot + jnp.where(ok, shifted, 0.0)
            lo = jnp.maximum(t - win // 2, s0)
            hi = jnp.minimum(t + (win - win // 2), s1)
            cnt = (hi - lo).astype(F32)
            pooled = (tot * (1.0 / cnt) - a).astype(BF16)
            o_ref[0] = (_dot(pooled, w_ref[0]) * s_ref[0]).astype(o_ref.dtype)


def _pool_mixer(p, pool_w, pool_scale, seq):
    b, t, _ = p.shape
    g, gw, _ = pool_w.shape
    return pl.pallas_call(
        functools.partial(_pool_kernel, seq=seq, t_all=t),
        grid=(b, g),
        in_specs=[pl.BlockSpec((1, t, gw), lambda b_, g_: (b_, 0, g_)),
                  pl.BlockSpec((1, gw, gw), lambda b_, g_: (g_, 0, 0)),
                  pl.BlockSpec((1, 1, gw), lambda b_, g_: (g_, 0, 0))],
        out_specs=pl.BlockSpec((1, t, gw), lambda b_, g_: (b_, 0, g_)),
        out_shape=jax.ShapeDtypeStruct((b, t, g * gw), BF16),
        compiler_params=_cparams(("parallel", "parallel")),
        name="pool_mixer",
    )(p, pool_w, pool_scale.reshape(g, 1, gw))


def _gate_values(raw, a_log, dt, is_beta):
    x = raw + dt
    softplus = jnp.maximum(x, 0.0) + jnp.log(1.0 + jnp.exp(-jnp.abs(x)))
    return jnp.where(is_beta, _sigmoid(raw), -jnp.exp(a_log) * softplus)


GDN_GROUP = 4


def _gdn_pre_kernel(q_ref, k_ref, v_ref, gc_ref, gr_ref, pc_ref, pr_ref, cwq_ref, cwk_ref, cwv_ref,
                    u_ref, wq_ref, kd_ref, a_ref, eg_ref, *, seq, t_all):
    c = GDN_CHUNK
    gsz = GDN_GROUP
    rows = gsz * c
    n_groups = t_all // rows
    seq_g = seq // rows
    ri = lax.broadcasted_iota(jnp.int32, (c, c), 0)
    ci = lax.broadcasted_iota(jnp.int32, (c, c), 1)
    lower = ci <= ri
    upper = ci >= ri
    eye = (ci == ri).astype(F32)
    row = lax.broadcasted_iota(jnp.int32, (rows, 1), 0)
    lane4 = lax.broadcasted_iota(jnp.int32, (rows, 4), 1)
    sub4 = lax.broadcasted_iota(jnp.int32, (4, c), 0)
    pc = pc_ref[0]
    pr = pr_ref[0]

    def group(gi, carry):
        r0 = pl.multiple_of(gi * rows, rows)
        n0 = gi * gsz
        has_prev = jnp.where((gi != 0) & (gi != seq_g), 1.0, 0.0).astype(F32)
        has_next = jnp.where((gi != seq_g - 1) & (gi != n_groups - 1), 1.0, 0.0).astype(F32)
        pidx = jnp.maximum(r0 - 1, 0)
        nidx = jnp.minimum(r0 + rows, t_all - 1)

        def conv_silu(ref, cw_ref):
            xm = ref[0, pl.ds(r0, rows), :]
            xp = jnp.where(row == 0, ref[0, pl.ds(pidx, 1), :] * has_prev, pltpu.roll(xm, 1, axis=0))
            xn = jnp.where(row == rows - 1, ref[0, pl.ds(nidx, 1), :] * has_next, pltpu.roll(xm, rows - 1, axis=0))
            cw = cw_ref[...]
            return _silu(cw[0:1] * xp + cw[1:2] * xm + cw[2:3] * xn)

        def l2n(x):
            return x * lax.rsqrt(jnp.sum(x * x, axis=-1, keepdims=True) + EPS)

        q_all = l2n(conv_silu(q_ref, cwq_ref)) * (GDN_DK ** -0.5)
        k_all = l2n(conv_silu(k_ref, cwk_ref))
        v_all = conv_silu(v_ref, cwv_ref)
        gcol_all = _gate_values(gc_ref[0, 0, pl.ds(r0, rows), :], pc[0:1], pc[1:2], lane4 < 2)
        chunk = lambda x, j: x[j * c:(j + 1) * c]
        qs = [chunk(q_all, j) for j in range(gsz)]
        ks = [chunk(k_all, j) for j in range(gsz)]
        vs = [chunk(v_all, j) for j in range(gsz)]
        kbs = [k.astype(BF16) for k in ks]
        qks = [_dot_nt(q.astype(BF16), kb) for q, kb in zip(qs, kbs)]
        kks = [_dot_nt(kb, kb) for kb in kbs]
        chains = []
        for j in range(gsz):
            gcol = chunk(gcol_all, j)
            grow = _gate_values(gr_ref[0, 0, n0 + j], pr[:, 0:1], pr[:, 1:2], sub4 < 2)
            for d in range(2):
                beta = gcol[:, d:d + 1]
                g_c = gcol[:, 2 + d:3 + d]
                g_r = grow[2 + d:3 + d, :]
                incl = lower if d == 0 else upper
                strict = (ci < ri) if d == 0 else (ci > ri)
                inclt = upper if d == 0 else lower
                gcc = jnp.sum(jnp.where(incl, g_r, 0.0), axis=1, keepdims=True)
                gcr = jnp.sum(jnp.where(inclt, g_c, 0.0), axis=0, keepdims=True)
                decay = jnp.exp(jnp.where(incl, gcc - gcr, -jnp.inf))
                g_last = gcc[c - 1:c] if d == 0 else gcc[0:1]
                lmat = jnp.where(strict, kks[j] * beta * decay, 0.0)
                egc = jnp.exp(gcc)
                rhs = jnp.concatenate([vs[j] * beta, ks[j] * (beta * egc)], axis=1).astype(BF16)
                chains.append(dict(j=j, d=d, lmat=lmat, rhs=rhs, qg=qs[j] * egc,
                                   kd=(ks[j] * jnp.exp(g_last - gcc)).astype(BF16),
                                   amat=jnp.where(incl, qks[j] * decay, 0.0).astype(BF16),
                                   eg=jnp.broadcast_to(jnp.exp(g_last), (8, LANES))))
        tmats = [eye - ch["lmat"] for ch in chains]
        pws = [ch["lmat"] for ch in chains]
        for _ in range(int(math.log2(c)) - 1):
            pws = [_dot(pw.astype(BF16), pw.astype(BF16)) for pw in pws]
            tmats = [tm + _dot(tm.astype(BF16), pw.astype(BF16)) for tm, pw in zip(tmats, pws)]
        uws = [_dot(tm.astype(BF16), ch["rhs"]) for tm, ch in zip(tmats, chains)]
        for uw, ch in zip(uws, chains):
            d, n = ch["d"], n0 + ch["j"]
            u_ref[0, d, n, 0] = uw[:, :GDN_DV]
            wq_ref[0, d, n, 0] = jnp.concatenate([uw[:, GDN_DV:], ch["qg"]], axis=0).astype(BF16)
            kd_ref[0, d, n, 0] = ch["kd"]
            a_ref[0, d, n, 0] = ch["amat"]
            eg_ref[0, d, n, 0] = ch["eg"]
        return carry

    lax.fori_loop(0, n_groups, group, 0)


def _gdn_rec_kernel(uf, ub, wqf, wqb, kdf, kdb, af, ab, egf, egb, of_ref, ob_ref, s_scr):
    c = GDN_CHUNK
    heads = range(GDN_HEADS)

    @pl.when(pl.program_id(1) == 0)
    def _():
        s_scr[...] = jnp.zeros_like(s_scr)

    streams = ((uf, wqf, kdf, af, egf, of_ref), (ub, wqb, kdb, ab, egb, ob_ref))
    for d, (u, wq, kd, a, eg, o_ref) in enumerate(streams):
        sts = [s_scr[d, h] for h in heads]
        wqs = [_dot(wq[0, 0, 0, h], sts[h].astype(BF16)) for h in heads]
        vbs = [(u[0, 0, 0, h] - wqs[h][:c]).astype(BF16) for h in heads]
        for h in heads:
            o_ref[0, :, h * GDN_DV:(h + 1) * GDN_DV] = wqs[h][c:] + _dot(a[0, 0, 0, h], vbs[h])
        for h in heads:
            s_scr[d, h] = sts[h] * eg[0, 0, 0, h][0:1] + _dot_tn(kd[0, 0, 0, h], vbs[h])


def _gdn_mixer(p, gates, conv_w, a_log, dt_bias, seq):
    b, t, _ = p.shape
    h, c = GDN_HEADS, GDN_CHUNK
    nc = t // c
    seq_c = seq // c
    ctx_c = nc - seq_c
    assert seq % (GDN_GROUP * c) == 0 and t % (GDN_GROUP * c) == 0
    pool_blocks = (p.shape[2] - 4 * h * GDN_DK) // LANES
    gates = gates[..., :4 * h].reshape(b, t, 4, h)
    g_col = gates.transpose(0, 3, 1, 2)
    g_row = gates.reshape(b, nc, c, 4, h).transpose(0, 4, 1, 3, 2)
    zeros = jnp.zeros((2, h), F32)
    prm = jnp.concatenate([zeros, a_log, zeros, dt_bias], axis=0).reshape(2, 4, h).astype(F32)
    p_col = prm.transpose(2, 0, 1)
    p_row = prm.transpose(2, 1, 0)
    col = lambda off: pl.BlockSpec((1, t, LANES), lambda b_, h_: (b_, 0, pool_blocks + off * h + h_))
    cw = lambda off: pl.BlockSpec((3, LANES), lambda b_, h_: (0, off * h + h_))
    pre_shapes = [((c, GDN_DV), F32), ((2 * c, GDN_DK), BF16), ((c, GDN_DK), BF16), ((c, c), BF16), ((8, LANES), F32)]
    pre = pl.pallas_call(
        functools.partial(_gdn_pre_kernel, seq=seq, t_all=t),
        grid=(b, h),
        in_specs=[col(0), col(1), col(2),
                  pl.BlockSpec((1, 1, t, 4), lambda b_, h_: (b_, h_, 0, 0)),
                  pl.BlockSpec((1, 1, nc, 4, c), lambda b_, h_: (b_, h_, 0, 0, 0)),
                  pl.BlockSpec((1, 2, 4), lambda b_, h_: (h_, 0, 0)),
                  pl.BlockSpec((1, 4, 2), lambda b_, h_: (h_, 0, 0)),
                  cw(0), cw(1), cw(2)],
        out_specs=[pl.BlockSpec((1, 2, nc, 1) + s, lambda b_, h_: (b_, 0, 0, h_, 0, 0)) for s, _ in pre_shapes],
        out_shape=[jax.ShapeDtypeStruct((b, 2, nc, h) + s, dt) for s, dt in pre_shapes],
        compiler_params=_cparams(("parallel", "parallel")),
        name="gated_delta_pre",
    )(p, p, p, g_col, g_row, p_col, p_row, conv_w, conv_w, conv_w)

    fwd_chunk = lambda s: jnp.where(s < ctx_c, seq_c + s, s - ctx_c)
    bwd_chunk = lambda s: nc - 1 - s
    in_specs, args = [], []
    for arr, (s_, _) in zip(pre, pre_shapes):
        for d, chunk in enumerate((fwd_chunk, bwd_chunk)):
            in_specs.append(pl.BlockSpec((1, 1, 1, h) + s_, lambda b_, s, d=d, chunk=chunk: (b_, d, chunk(s), 0, 0, 0)))
            args.append(arr)
    return pl.pallas_call(
        _gdn_rec_kernel,
        grid=(b, nc),
        in_specs=in_specs,
        out_specs=[pl.BlockSpec((1, c, h * GDN_DV), lambda b_, s: (b_, fwd_chunk(s), 0)),
                   pl.BlockSpec((1, c, h * GDN_DV), lambda b_, s: (b_, bwd_chunk(s), 0))],
        out_shape=[jax.ShapeDtypeStruct((b, t, h * GDN_DV), F32)] * 2,
        scratch_shapes=[pltpu.VMEM((2, h, GDN_DK, GDN_DV), F32)],
        compiler_params=_cparams(("parallel", "arbitrary")),
        name="gated_delta_scan",
    )(*args)


ATTN_SUB = 256


def _attn_kernel(q_ref, qc_ref, k_ref, v_ref, lam_ref, g_ref, o_ref, *, seq, t_all, tq, lam_init):
    i = pl.program_id(2)
    hd = DIFF_HD
    n_ctx = t_all - seq
    lp = lam_ref[...]
    lam = (jnp.exp(jnp.sum(lp[0:1] * lp[1:2], axis=-1, keepdims=True))
           - jnp.exp(jnp.sum(lp[2:3] * lp[3:4], axis=-1, keepdims=True)) + lam_init)

    def attend(qs, k, v):
        ss = [_dot_nt(q[:, cpt * hd:(cpt + 1) * hd], k[:, cpt * hd:(cpt + 1) * hd]) for q in qs for cpt in range(2)]
        es = [jnp.exp2(s - jnp.max(s, axis=-1, keepdims=True)) for s in ss]
        inv = [1.0 / jnp.sum(e, axis=-1, keepdims=True) for e in es]
        pvs = [_dot(e.astype(BF16), v) for e in es]
        outs = []
        for r in range(len(qs)):
            o = pvs[2 * r] * inv[2 * r] - lam * (pvs[2 * r + 1] * inv[2 * r + 1])
            o = o * lax.rsqrt(jnp.mean(o * o, axis=-1, keepdims=True) + EPS) * g_ref[...] * (1.0 - lam_init)
            outs.append(o.astype(o_ref.dtype))
        return outs

    @pl.when(i * tq < seq)
    def _():
        outs = attend([q_ref[0, r:r + ATTN_SUB, :] for r in range(0, tq, ATTN_SUB)], k_ref[0], v_ref[0])
        for r, o in zip(range(0, tq, ATTN_SUB), outs):
            o_ref[0, r:r + ATTN_SUB, :] = o

    @pl.when(i * tq >= seq)
    def _():
        o_ref[0, 0:n_ctx, :] = attend([qc_ref[0]], k_ref[0, seq:t_all, :], v_ref[0, seq:t_all, :])[0]


def _diff_attention(qkv, lam_p, subln_g, seq, lam_init):
    b, t, _ = qkv.shape
    h, w = DIFF_HEADS, 2 * DIFF_HD
    n_ctx = t - seq
    tq = _pick(seq, (512, 256))
    assert tq % ATTN_SUB == 0 and 0 < n_ctx <= tq and seq % n_ctx == 0 and n_ctx % 16 == 0
    n_lat = seq // tq
    return pl.pallas_call(
        functools.partial(_attn_kernel, seq=seq, t_all=t, tq=tq, lam_init=lam_init),
        grid=(b, h, n_lat + 1),
        in_specs=[pl.BlockSpec((1, tq, w), lambda b_, h_, i: (b_, jnp.minimum(i, n_lat - 1), h_)),
                  pl.BlockSpec((1, n_ctx, w), lambda b_, h_, i: (b_, seq // n_ctx, h_)),
                  pl.BlockSpec((1, t, w), lambda b_, h_, i: (b_, 0, h + h_)),
                  pl.BlockSpec((1, t, w), lambda b_, h_, i: (b_, 0, 2 * h + h_)),
                  pl.BlockSpec((4, DIFF_HD), lambda b_, h_, i: (0, 0)),
                  pl.BlockSpec((1, w), lambda b_, h_, i: (0, 0))],
        out_specs=pl.BlockSpec((1, tq, w), lambda b_, h_, i: (b_, i, h_)),
        out_shape=jax.ShapeDtypeStruct((b, t, h * w), BF16),
        compiler_params=_cparams(("parallel", "parallel", "arbitrary")),
        name="diff_attention",
    )(qkv, qkv, qkv, qkv, lam_p, subln_g.reshape(1, w))


def _rope_tables(seq, t_all):
    rows = seq // GRID_W
    row = jnp.repeat(jnp.arange(rows), GRID_W).astype(F32)
    col = jnp.tile(jnp.arange(GRID_W), rows).astype(F32)
    n_freq = DIFF_HD // 4
    inv = ROPE_THETA ** (-jnp.arange(n_freq, dtype=F32) / n_freq)
    ang = jnp.concatenate([row[:, None] * inv, col[:, None] * inv], axis=-1)
    cos = jnp.concatenate([jnp.cos(ang), jnp.cos(ang)], axis=-1)
    sin = jnp.concatenate([-jnp.sin(ang), jnp.sin(ang)], axis=-1)
    pad = t_all - seq
    cos = jnp.concatenate([cos, jnp.ones((pad, DIFF_HD), F32)], axis=0)
    sin = jnp.concatenate([sin, jnp.zeros((pad, DIFF_HD), F32)], axis=0)
    return cos, sin


def kernel(x, c, ctx, c_ctx, w_mod, b_mod, ln1_g, ln1_b, ln2_g, ln2_b, ev_w_in, ev_conv, ev_a_log, ev_dt_bias, ev_pool_w, ev_pool_scale, ev_norm, ev_w_out, od_w_in, od_lam_q1, od_lam_k1, od_lam_q2, od_lam_k2, od_subln, od_w_out, ffn_w_up, ffn_conv, ffn_w_down):
    b, seq, d = x.shape
    t_all = seq + ctx.shape[1]
    depth = w_mod.shape[0]
    alpha = (2 * depth) ** 0.25
    pool_width = ev_pool_w.shape[1] * ev_pool_w.shape[2]
    gdn_main = pool_width + 4 * GDN_HEADS * GDN_DK
    diff_qw = DIFF_HEADS * 2 * DIFF_HD

    r_pad = -(-(b + 1) // 8) * 8
    cond = jnp.concatenate([c, c_ctx[None, :], jnp.zeros((r_pad - b - 1, d), F32)], axis=0)
    mods = _modulation(cond, w_mod, b_mod).reshape(depth, r_pad, 6, d)
    lat = mods[:, :b]
    cx = jnp.broadcast_to(mods[:, b:b + 1], lat.shape)
    zeros2 = jnp.zeros((depth, b, 2, d), F32)
    mod1 = jnp.concatenate([lat[:, :, 0:3], cx[:, :, 0:3], zeros2], axis=2)
    mod2 = jnp.concatenate([lat[:, :, 3:6], cx[:, :, 3:6], zeros2], axis=2)

    cos, sin = _rope_tables(seq, t_all)
    xa = jnp.concatenate([x, ctx], axis=1)

    for l in range(depth):
        i = l // 2
        if l % 2 == 0:
            w_in = ev_w_in[i].astype(BF16)
            w_gate = jnp.pad(w_in[:, gdn_main:], ((0, 0), (0, LANES - (w_in.shape[1] - gdn_main))))
            p = _mm1(xa, mod1[l], w_in[:, :gdn_main], seq, F32)
            gates = _mm1(xa, mod1[l], w_gate, seq, F32)
            ya = _pool_mixer(p, ev_pool_w[i].astype(BF16), ev_pool_scale[i], seq)
            o_f, o_b = _gdn_mixer(p, gates, ev_conv[i], ev_a_log[i], ev_dt_bias[i], seq)
            w_out = ev_w_out[i].astype(BF16)
            z_blk = (gdn_main - o_f.shape[2]) // o_f.shape[2]
            xa = _mm2([ya], [w_out[:pool_width], w_out[pool_width:]], xa, mod1[l], ln1_g[l], ln1_b[l], seq, alpha,
                      gdn=(o_f, o_b, p, z_blk, ev_norm[i]))
        else:
            lam_init = 0.8 - 0.6 * math.exp(-0.3 * l)
            qkv = _mm1(xa, mod1[l], od_w_in[i].astype(BF16), seq, BF16,
                       rope=(cos, sin, 2 * diff_qw, diff_qw, DIFF_HD ** -0.5 * math.log2(math.e)))
            lam_p = jnp.stack([od_lam_q1[i], od_lam_k1[i], od_lam_q2[i], od_lam_k2[i]]).astype(F32)
            y = _diff_attention(qkv, lam_p, od_subln[i], seq, lam_init)
            xa = _mm2([y], [od_w_out[i].astype(BF16)], xa, mod1[l], ln1_g[l], ln1_b[l], seq, alpha)
        xa = _ffn(xa, mod2[l], ffn_w_up[l].astype(BF16), ffn_conv[l], ffn_w_down[l].astype(BF16),
                  ln2_g[l], ln2_b[l], seq, alpha, seq if l == depth - 1 else t_all)
    return xa
```

```python
import functools
import math

import jax
import jax.numpy as jnp
from jax import lax
from jax.experimental import pallas as pl
from jax.experimental.pallas import tpu as pltpu

F32 = jnp.float32
BF16 = jnp.bfloat16

GRID_W = 64
POOL_GROUPS = 4
POOL_WINDOWS = (2, 4, 8, 16)
GDN_HEADS = 8
GDN_DK = 128
GDN_DV = 128
GDN_CHUNK = 64
DIFF_HEADS = 8
DIFF_HD = 128
ROPE_THETA = 10000.0
EPS = 1e-6

V7X_VMEM_BYTES = 64 * 1024 * 1024
VMEM_LIMIT = V7X_VMEM_BYTES - 8 * 1024 * 1024
LANES = 128
HALO = 16


def _cparams(sem):
    return pltpu.CompilerParams(dimension_semantics=sem, vmem_limit_bytes=VMEM_LIMIT)


def _sigmoid(x):
    return 1.0 / (1.0 + jnp.exp(-x))


def _silu(x):
    return x * _sigmoid(x)


def _dot(a, b):
    return jnp.dot(a, b, preferred_element_type=F32)


def _dot_nt(a, b):
    return lax.dot_general(a, b, (((1,), (1,)), ((), ())), preferred_element_type=F32)


def _dot_tn(a, b):
    return lax.dot_general(a, b, (((0,), (0,)), ((), ())), preferred_element_type=F32)


def _layer_norm_rows(r, g, b):
    mu = jnp.mean(r, axis=-1, keepdims=True)
    d = r - mu
    var = jnp.mean(d * d, axis=-1, keepdims=True)
    return d * lax.rsqrt(var + EPS) * g + b


def _pick(n, cands):
    for c in cands:
        if n % c == 0:
            return c
    raise ValueError(f"no tile in {cands} divides {n}")


def _mod_kernel(c_ref, w_ref, b_ref, o_ref):
    s = _silu(c_ref[...]).astype(BF16)
    o_ref[0] = _dot(s, w_ref[0].astype(BF16)) + b_ref[0]


def _modulation(cond, w_mod, b_mod):
    depth, d, n = w_mod.shape
    r = cond.shape[0]
    tn = _pick(n, (1024, 512, 256, 128))
    return pl.pallas_call(
        _mod_kernel,
        grid=(depth, n // tn),
        in_specs=[pl.BlockSpec((r, d), lambda l, j: (0, 0)),
                  pl.BlockSpec((1, d, tn), lambda l, j: (l, 0, j)),
                  pl.BlockSpec((1, 1, tn), lambda l, j: (l, 0, j))],
        out_specs=pl.BlockSpec((1, r, tn), lambda l, j: (l, 0, j)),
        out_shape=jax.ShapeDtypeStruct((depth, r, n), F32),
        compiler_params=_cparams(("parallel", "parallel")),
        name="modulation",
    )(cond, w_mod, b_mod.reshape(depth, 1, n))


def _per_segment(i, tm, seq, fn):
    i_mix, split = divmod(seq, tm)
    assert split % 16 == 0

    @pl.when(i < i_mix)
    def _():
        fn(slice(0, tm), 0)

    @pl.when(i > i_mix)
    def _():
        fn(slice(0, tm), 1)

    @pl.when(i == i_mix)
    def _():
        if split:
            fn(slice(0, split), 0)
        fn(slice(split, tm), 1)


def _mm1_kernel(*refs, seq, tm, rope_tiles, q_tiles, q_scale, gate_act):
    if rope_tiles:
        x_ref, mod_ref, w_ref, cos_ref, sin_ref, o_ref, u_scr = refs
    elif gate_act:
        x_ref, mod_ref, w_ref, ga_ref, o_ref, u_scr = refs
    else:
        x_ref, mod_ref, w_ref, o_ref, u_scr = refs
    i = pl.program_id(1)
    j = pl.program_id(2)

    @pl.when(j == 0)
    def _():
        def modulate(rs, k):
            m = mod_ref[0]
            u_scr[rs] = (x_ref[0, rs, :] * m[3 * k + 1:3 * k + 2] + m[3 * k:3 * k + 1]).astype(BF16)

        _per_segment(i, tm, seq, modulate)

    acc = _dot(u_scr[...], w_ref[...])
    if gate_act:
        ga = ga_ref[...]
        xg = acc + ga[1:2]
        softplus = jnp.maximum(xg, 0.0) + jnp.log(1.0 + jnp.exp(-jnp.abs(xg)))
        o_ref[0] = jnp.where(ga[2:3] > 0.5, _sigmoid(acc), -jnp.exp(ga[0:1]) * softplus)
        return
    if not rope_tiles:
        o_ref[0] = acc.astype(o_ref.dtype)
        return

    @pl.when(j < rope_tiles)
    def _():
        sc = jnp.where(j < q_tiles, q_scale, 1.0).astype(F32)
        cos = cos_ref[...] * sc
        sin = sin_ref[...] * sc
        for g in range(acc.shape[1] // LANES):
            blk = acc[:, g * LANES:(g + 1) * LANES]
            rot = pltpu.roll(blk, LANES // 2, axis=1)
            o_ref[0, :, g * LANES:(g + 1) * LANES] = (blk * cos + rot * sin).astype(o_ref.dtype)

    @pl.when(j >= rope_tiles)
    def _():
        o_ref[0] = acc.astype(o_ref.dtype)


def _mm1(xa, mod, w, seq, out_dtype, rope=None, gate_act=None):
    b, t, d = xa.shape
    n = w.shape[1]
    tm = _pick(t, (768, 512, 256, 128))
    tn = _pick(n, (1024, 512, 256, 128))
    in_specs = [pl.BlockSpec((1, tm, d), lambda b_, i, j: (b_, i, 0)),
                pl.BlockSpec((1, 8, d), lambda b_, i, j: (b_, 0, 0)),
                pl.BlockSpec((d, tn), lambda b_, i, j: (0, j))]
    args = [xa, mod, w]
    kw = dict(seq=seq, tm=tm, rope_tiles=0, q_tiles=0, q_scale=1.0, gate_act=gate_act is not None)
    if gate_act is not None:
        in_specs += [pl.BlockSpec((3, tn), lambda b_, i, j: (0, j))]
        args += [gate_act]
    if rope is not None:
        cos, sin, rope_cols, q_cols, q_scale = rope
        assert rope_cols % tn == 0 and q_cols % tn == 0
        in_specs += [pl.BlockSpec((tm, LANES), lambda b_, i, j: (i, 0)),
                     pl.BlockSpec((tm, LANES), lambda b_, i, j: (i, 0))]
        args += [cos, sin]
        kw.update(rope_tiles=rope_cols // tn, q_tiles=q_cols // tn, q_scale=q_scale)
    return pl.pallas_call(
        functools.partial(_mm1_kernel, **kw),
        grid=(b, t // tm, n // tn),
        in_specs=in_specs,
        out_specs=pl.BlockSpec((1, tm, tn), lambda b_, i, j: (b_, i, j)),
        out_shape=jax.ShapeDtypeStruct((b, t, n), out_dtype),
        scratch_shapes=[pltpu.VMEM((tm, d), BF16)],
        compiler_params=_cparams(("parallel", "parallel", "arbitrary")),
        name="mod_in_proj",
    )(*args)


def _mm2_kernel(*refs, gdn, seq, tm, alpha):
    if gdn:
        ya_ref, of_ref, ob_ref, z_ref, ng_ref, wa_ref, wb_ref, x_ref, mod_ref, g_ref, b_ref, o_ref, yb_scr = refs
        for h in range(GDN_HEADS):
            sl = slice(h * GDN_DV, (h + 1) * GDN_DV)
            o = of_ref[0, :, sl] + ob_ref[0, :, sl]
            o = o * lax.rsqrt(jnp.mean(o * o, axis=-1, keepdims=True) + EPS) * ng_ref[...]
            yb_scr[:, sl] = (o * _silu(z_ref[0, :, sl])).astype(BF16)
        acc = _dot(ya_ref[0], wa_ref[...]) + _dot(yb_scr[...], wb_ref[...])
    else:
        y_ref, w_ref, x_ref, mod_ref, g_ref, b_ref, o_ref = refs
        acc = _dot(y_ref[0], w_ref[...])

    def finish(rs, k):
        r = alpha * x_ref[0, rs, :] + mod_ref[0][3 * k + 2:3 * k + 3] * acc[rs]
        o_ref[0, rs, :] = _layer_norm_rows(r, g_ref[...], b_ref[...])

    _per_segment(pl.program_id(1), tm, seq, finish)


def _mm2(ys, ws, xa, mod, ln_g, ln_b, seq, alpha, gdn=None):
    b, t, d = xa.shape
    tm = _pick(t, (384, 256, 128))
    row_blk = lambda width, cb=0: pl.BlockSpec((1, tm, width), lambda b_, i: (b_, i, cb))
    full = lambda a: pl.BlockSpec(a.shape, lambda b_, i: (0,) * a.ndim)
    in_specs = [row_blk(ys[0].shape[2])]
    args = [ys[0]]
    scratch = []
    if gdn is not None:
        o_f, o_b, p, z_blk, norm_g = gdn
        zw = o_f.shape[2]
        in_specs += [row_blk(zw), row_blk(zw), row_blk(zw, z_blk), pl.BlockSpec((1, GDN_DV), lambda b_, i: (0, 0))]
        args += [o_f, o_b, p, norm_g.reshape(1, GDN_DV)]
        scratch = [pltpu.VMEM((tm, zw), BF16)]
    in_specs += [full(w) for w in ws]
    in_specs += [row_blk(d), pl.BlockSpec((1, 8, d), lambda b_, i: (b_, 0, 0)),
                 pl.BlockSpec((1, d), lambda b_, i: (0, 0)), pl.BlockSpec((1, d), lambda b_, i: (0, 0))]
    return pl.pallas_call(
        functools.partial(_mm2_kernel, gdn=gdn is not None, seq=seq, tm=tm, alpha=alpha),
        grid=(b, t // tm),
        in_specs=in_specs,
        out_specs=row_blk(d),
        out_shape=jax.ShapeDtypeStruct((b, t, d), F32),
        scratch_shapes=scratch,
        compiler_params=_cparams(("parallel", "parallel")),
        name="out_proj_ln",
    )(*args, *ws, xa, mod, ln_g.reshape(1, d), ln_b.reshape(1, d))


def _ffn_kernel(x_ref, xp_ref, xn_ref, mod_ref, wg_ref, wu_ref, cw_ref, wd_ref, g_ref, b_ref, o_ref,
                u_scr, acc_scr, *, seq, t_all, tm, alpha):
    i = pl.program_id(1)
    j = pl.program_id(2)
    nj = pl.num_programs(2)
    row0 = i * tm

    @pl.when(j == 0)
    def _():
        m = mod_ref[0]

        def modulate(rs, k):
            u_scr[HALO + rs.start:HALO + rs.stop] = (
                x_ref[0, rs, :] * m[3 * k + 1:3 * k + 2] + m[3 * k:3 * k + 1]).astype(BF16)

        def halo(x, in_latent):
            shift = jnp.where(in_latent, m[0:1], m[3:4])
            scale1p = jnp.where(in_latent, m[1:2], m[4:5])
            return (x * scale1p + shift).astype(BF16)

        u_scr[0:HALO] = halo(xp_ref[0], row0 < seq)
        _per_segment(i, tm, seq, modulate)
        u_scr[HALO + tm:2 * HALO + tm] = halo(xn_ref[0], row0 + tm - 1 < seq)
        acc_scr[...] = jnp.zeros_like(acc_scr)

    gate_ext = _dot(u_scr[...], wg_ref[...])
    up = _dot(u_scr[HALO:HALO + tm], wu_ref[...])
    gm = gate_ext[HALO:HALO + tm]
    lrow = lax.broadcasted_iota(jnp.int32, (tm, 1), 0)
    grow = row0 + lrow
    prev = jnp.where(lrow == 0, gate_ext[HALO - 1:HALO], pltpu.roll(gm, 1, axis=0))
    nxt = jnp.where(lrow == tm - 1, gate_ext[HALO + tm:HALO + tm + 1], pltpu.roll(gm, tm - 1, axis=0))
    prev = jnp.where((grow == 0) | (grow == seq), 0.0, prev)
    nxt = jnp.where((grow == seq - 1) | (grow == t_all - 1), 0.0, nxt)
    cw = cw_ref[...]
    conv = cw[0:1] * prev + cw[1:2] * gm + cw[2:3] * nxt
    h = (_silu(conv) * up).astype(BF16)
    acc_scr[...] += _dot(h, wd_ref[...])

    @pl.when(j == nj - 1)
    def _():
        def finish(rs, k):
            r = alpha * x_ref[0, rs, :] + mod_ref[0][3 * k + 2:3 * k + 3] * acc_scr[rs]
            o_ref[0, rs, :] = _layer_norm_rows(r, g_ref[...], b_ref[...])

        _per_segment(i, tm, seq, finish)


def _ffn(xa, mod, w_up, conv_w, w_down, ln_g, ln_b, seq, alpha, out_rows):
    b, t, d = xa.shape
    dff = w_down.shape[0]
    tm = _pick(t, (768, 512, 256, 128))
    tf = _pick(dff, (512, 256, 128))
    nj = dff // tf
    hb = tm // HALO
    last_hb = t // HALO - 1
    return pl.pallas_call(
        functools.partial(_ffn_kernel, seq=seq, t_all=t, tm=tm, alpha=alpha),
        grid=(b, pl.cdiv(out_rows, tm), nj),
        in_specs=[pl.BlockSpec((1, tm, d), lambda b_, i, j: (b_, i, 0)),
                  pl.BlockSpec((1, HALO, d), lambda b_, i, j: (b_, jnp.maximum(i * hb - 1, 0), 0)),
                  pl.BlockSpec((1, HALO, d), lambda b_, i, j: (b_, jnp.minimum((i + 1) * hb, last_hb), 0)),
                  pl.BlockSpec((1, 8, d), lambda b_, i, j: (b_, 0, 0)),
                  pl.BlockSpec((d, tf), lambda b_, i, j: (0, j)),
                  pl.BlockSpec((d, tf), lambda b_, i, j: (0, nj + j)),
                  pl.BlockSpec((3, tf), lambda b_, i, j: (0, j)),
                  pl.BlockSpec((tf, d), lambda b_, i, j: (j, 0)),
                  pl.BlockSpec((1, d), lambda b_, i, j: (0, 0)),
                  pl.BlockSpec((1, d), lambda b_, i, j: (0, 0))],
        out_specs=pl.BlockSpec((1, tm, d), lambda b_, i, j: (b_, i, 0)),
        out_shape=jax.ShapeDtypeStruct((b, out_rows, d), F32),
        scratch_shapes=[pltpu.VMEM((tm + 2 * HALO, d), BF16), pltpu.VMEM((tm, d), F32)],
        compiler_params=_cparams(("parallel", "parallel", "arbitrary")),
        name="conv_glu_ffn",
    )(xa, xa, xa, mod, w_up, w_up, conv_w, w_down, ln_g.reshape(1, d), ln_b.reshape(1, d))


def _pool_kernel(a_ref, w_ref, s_ref, o_ref, *, seq, t_all):
    g = pl.program_id(1)
    a = a_ref[0].astype(F32)
    t = lax.broadcasted_iota(jnp.int32, (t_all, 1), 0)
    s0 = jnp.where(t < seq, 0, seq)
    s1 = jnp.where(t < seq, seq, t_all)
    for gi, win in enumerate(POOL_WINDOWS):
        @pl.when(g == gi)
        def _(win=win):
            tot = a
            for dlt in range(-(win // 2), win - win // 2):
                if dlt == 0:
                    continue
                shifted = pltpu.roll(a, (-dlt) % t_all, axis=0)
                ok = (t + dlt >= s0) & (t + dlt < s1)
                tot = tot + jnp.where(ok, shifted, 0.0)
            lo = jnp.maximum(t - win // 2, s0)
            hi = jnp.minimum(t + (win - win // 2), s1)
            cnt = (hi - lo).astype(F32)
            pooled = (tot * (1.0 / cnt) - a).astype(BF16)
            o_ref[0] = (_dot(pooled, w_ref[0]) * s_ref[0]).astype(o_ref.dtype)


def _pool_mixer(p, pool_w, pool_scale, seq):
    b, t, _ = p.shape
    g, gw, _ = pool_w.shape
    return pl.pallas_call(
        functools.partial(_pool_kernel, seq=seq, t_all=t),
        grid=(b, g),
        in_specs=[pl.BlockSpec((1, t, gw), lambda b_, g_: (b_, 0, g_)),
                  pl.BlockSpec((1, gw, gw), lambda b_, g_: (g_, 0, 0)),
                  pl.BlockSpec((1, 1, gw), lambda b_, g_: (g_, 0, 0))],
        out_specs=pl.BlockSpec((1, t, gw), lambda b_, g_: (b_, 0, g_)),
        out_shape=jax.ShapeDtypeStruct((b, t, g * gw), BF16),
        compiler_params=_cparams(("parallel", "parallel")),
        name="pool_mixer",
    )(p, pool_w, pool_scale.reshape(g, 1, gw))


GDN_GROUP = 12


def _gdn_pre_kernel(q_ref, k_ref, v_ref, gc_ref, gr_ref, cwq_ref, cwk_ref, cwv_ref,
                    u_ref, wq_ref, kd_ref, a_ref, eg_ref, *, seq, t_all):
    c = GDN_CHUNK
    gsz = GDN_GROUP
    rows = gsz * c
    n_groups = t_all // rows
    ri = lax.broadcasted_iota(jnp.int32, (c, c), 0)
    ci = lax.broadcasted_iota(jnp.int32, (c, c), 1)
    lower = ci <= ri
    upper = ci >= ri
    eye = (ci == ri).astype(F32)
    row = lax.broadcasted_iota(jnp.int32, (rows, 1), 0)

    def group(gi, carry):
        r0 = pl.multiple_of(gi * rows, rows)
        n0 = gi * gsz
        grow_idx = r0 + row
        no_prev = (grow_idx == 0) | (grow_idx == seq)
        no_next = (grow_idx == seq - 1) | (grow_idx == t_all - 1)
        pidx = jnp.maximum(r0 - 1, 0)
        nidx = jnp.minimum(r0 + rows, t_all - 1)

        def conv_silu(ref, cw_ref):
            xm = ref[0, pl.ds(r0, rows), :]
            xp = jnp.where(row == 0, ref[0, pl.ds(pidx, 1), :], pltpu.roll(xm, 1, axis=0))
            xn = jnp.where(row == rows - 1, ref[0, pl.ds(nidx, 1), :], pltpu.roll(xm, rows - 1, axis=0))
            xp = jnp.where(no_prev, 0.0, xp)
            xn = jnp.where(no_next, 0.0, xn)
            cw = cw_ref[...]
            return _silu(cw[0:1] * xp + cw[1:2] * xm + cw[2:3] * xn)

        def l2n(x):
            return x * lax.rsqrt(jnp.sum(x * x, axis=-1, keepdims=True) + EPS)

        q_all = l2n(conv_silu(q_ref, cwq_ref)) * (GDN_DK ** -0.5)
        k_all = l2n(conv_silu(k_ref, cwk_ref))
        v_all = conv_silu(v_ref, cwv_ref)
        gcol_all = gc_ref[0, 0, pl.ds(r0, rows), :]
        chunk = lambda x, j: x[j * c:(j + 1) * c]
        qs = [chunk(q_all, j) for j in range(gsz)]
        ks = [chunk(k_all, j) for j in range(gsz)]
        vs = [chunk(v_all, j) for j in range(gsz)]
        kbs = [k.astype(BF16) for k in ks]
        qks = [_dot_nt(q.astype(BF16), kb) for q, kb in zip(qs, kbs)]
        kks = [_dot_nt(kb, kb) for kb in kbs]
        chains = []
        for j in range(gsz):
            gcol = chunk(gcol_all, j)
            grow = gr_ref[0, 0, n0 + j]
            for d in range(2):
                beta = gcol[:, d:d + 1]
                g_c = gcol[:, 2 + d:3 + d]
                g_r = grow[2 + d:3 + d, :]
                incl = lower if d == 0 else upper
                strict = (ci < ri) if d == 0 else (ci > ri)
                inclt = upper if d == 0 else lower
                gcc = jnp.sum(jnp.where(incl, g_r, 0.0), axis=1, keepdims=True)
                gcr = jnp.sum(jnp.where(inclt, g_c, 0.0), axis=0, keepdims=True)
                decay = jnp.exp(jnp.where(incl, gcc - gcr, -jnp.inf))
                g_last = gcc[c - 1:c] if d == 0 else gcc[0:1]
                lmat = jnp.where(strict, kks[j] * beta * decay, 0.0)
                egc = jnp.exp(gcc)
                rhs = jnp.concatenate([vs[j] * beta, ks[j] * (beta * egc)], axis=1).astype(BF16)
                chains.append(dict(j=j, d=d, lmat=lmat, rhs=rhs, qg=qs[j] * egc,
                                   kd=(ks[j] * jnp.exp(g_last - gcc)).astype(BF16),
                                   amat=jnp.where(incl, qks[j] * decay, 0.0).astype(BF16),
                                   eg=jnp.broadcast_to(jnp.exp(g_last), (8, LANES))))
        tmats = [eye - ch["lmat"] for ch in chains]
        pws = [ch["lmat"] for ch in chains]
        for _ in range(int(math.log2(c)) - 1):
            pws = [_dot(pw.astype(BF16), pw.astype(BF16)) for pw in pws]
            tmats = [tm + _dot(tm.astype(BF16), pw.astype(BF16)) for tm, pw in zip(tmats, pws)]
        uws = [_dot(tm.astype(BF16), ch["rhs"]) for tm, ch in zip(tmats, chains)]
        for uw, ch in zip(uws, chains):
            d, n = ch["d"], n0 + ch["j"]
            u_ref[0, d, n, 0] = uw[:, :GDN_DV]
            wq_ref[0, d, n, 0] = jnp.concatenate([uw[:, GDN_DV:], ch["qg"]], axis=0).astype(BF16)
            kd_ref[0, d, n, 0] = ch["kd"]
            a_ref[0, d, n, 0] = ch["amat"]
            eg_ref[0, d, n, 0] = ch["eg"]
        return carry

    lax.fori_loop(0, n_groups, group, 0)


GDN_SCAN_CHUNKS = 4


def _gdn_rec_kernel(uf, ub, wqf, wqb, kdf, kdb, af, ab, egf, egb, of_ref, ob_ref, s_scr):
    c = GDN_CHUNK
    ncs = GDN_SCAN_CHUNKS

    @pl.when(pl.program_id(1) == 0)
    def _():
        s_scr[...] = jnp.zeros_like(s_scr)

    streams = ((uf, wqf, kdf, af, egf, of_ref), (ub, wqb, kdb, ab, egb, ob_ref))
    ids = [(d, h) for d in range(2) for h in range(GDN_HEADS)]
    for t in range(ncs):
        cc = (t, ncs - 1 - t)
        sts = [s_scr[d, h] for d, h in ids]
        wqs = [_dot(streams[d][1][0, 0, cc[d], h], st.astype(BF16)) for (d, h), st in zip(ids, sts)]
        vbs = [(streams[d][0][0, 0, cc[d], h] - wq[:c]).astype(BF16) for (d, h), wq in zip(ids, wqs)]
        for (d, h), wq, vb in zip(ids, wqs, vbs):
            streams[d][5][0, cc[d] * c:(cc[d] + 1) * c, h * GDN_DV:(h + 1) * GDN_DV] = (
                wq[c:] + _dot(streams[d][3][0, 0, cc[d], h], vb))
        for (d, h), st, vb in zip(ids, sts, vbs):
            s_scr[d, h] = st * streams[d][4][0, 0, cc[d], h][0:1] + _dot_tn(streams[d][2][0, 0, cc[d], h], vb)


def _gdn_mixer(p, gates, conv_w, seq):
    b, t, _ = p.shape
    h, c = GDN_HEADS, GDN_CHUNK
    nc = t // c
    seq_c = seq // c
    ctx_c = nc - seq_c
    assert seq % c == 0 and t % (GDN_GROUP * c) == 0
    pool_blocks = (p.shape[2] - 4 * h * GDN_DK) // LANES
    gates = gates[..., :4 * h].reshape(b, t, 4, h)
    g_col = gates.transpose(0, 3, 1, 2)
    g_row = gates.reshape(b, nc, c, 4, h).transpose(0, 4, 1, 3, 2)
    col = lambda off: pl.BlockSpec((1, t, LANES), lambda b_, h_: (b_, 0, pool_blocks + off * h + h_))
    cw = lambda off: pl.BlockSpec((3, LANES), lambda b_, h_: (0, off * h + h_))
    pre_shapes = [((c, GDN_DV), F32), ((2 * c, GDN_DK), BF16), ((c, GDN_DK), BF16), ((c, c), BF16), ((8, LANES), F32)]
    pre = pl.pallas_call(
        functools.partial(_gdn_pre_kernel, seq=seq, t_all=t),
        grid=(b, h),
        in_specs=[col(0), col(1), col(2),
                  pl.BlockSpec((1, 1, t, 4), lambda b_, h_: (b_, h_, 0, 0)),
                  pl.BlockSpec((1, 1, nc, 4, c), lambda b_, h_: (b_, h_, 0, 0, 0)),
                  cw(0), cw(1), cw(2)],
        out_specs=[pl.BlockSpec((1, 2, nc, 1) + s, lambda b_, h_: (b_, 0, 0, h_, 0, 0)) for s, _ in pre_shapes],
        out_shape=[jax.ShapeDtypeStruct((b, 2, nc, h) + s, dt) for s, dt in pre_shapes],
        compiler_params=_cparams(("parallel", "parallel")),
        name="gated_delta_pre",
    )(p, p, p, g_col, g_row, conv_w, conv_w, conv_w)

    ncs = GDN_SCAN_CHUNKS
    assert seq_c % ncs == 0 and ctx_c % ncs == 0
    nblk, seq_b, ctx_b = nc // ncs, seq_c // ncs, ctx_c // ncs
    fwd_blk = lambda s: jnp.where(s < ctx_b, seq_b + s, s - ctx_b)
    bwd_blk = lambda s: nblk - 1 - s
    in_specs, args = [], []
    for arr, (s_, _) in zip(pre, pre_shapes):
        for d, blk in enumerate((fwd_blk, bwd_blk)):
            in_specs.append(pl.BlockSpec((1, 1, ncs, h) + s_, lambda b_, s, d=d, blk=blk: (b_, d, blk(s), 0, 0, 0)))
            args.append(arr)
    return pl.pallas_call(
        _gdn_rec_kernel,
        grid=(b, nblk),
        in_specs=in_specs,
        out_specs=[pl.BlockSpec((1, ncs * c, h * GDN_DV), lambda b_, s: (b_, fwd_blk(s), 0)),
                   pl.BlockSpec((1, ncs * c, h * GDN_DV), lambda b_, s: (b_, bwd_blk(s), 0))],
        out_shape=[jax.ShapeDtypeStruct((b, t, h * GDN_DV), F32)] * 2,
        scratch_shapes=[pltpu.VMEM((2, h, GDN_DK, GDN_DV), F32)],
        compiler_params=_cparams(("parallel", "arbitrary")),
        name="gated_delta_scan",
    )(*args)


ATTN_SUB = 256


def _attn_kernel(q_ref, qc_ref, k_ref, v_ref, lam_ref, g_ref, o_ref, *, seq, t_all, tq, lam_init):
    i = pl.program_id(2)
    hd = DIFF_HD
    n_ctx = t_all - seq
    lp = lam_ref[...]
    lam = (jnp.exp(jnp.sum(lp[0:1] * lp[1:2], axis=-1, keepdims=True))
           - jnp.exp(jnp.sum(lp[2:3] * lp[3:4], axis=-1, keepdims=True)) + lam_init)

    def attend(qs, k, v):
        ss = [_dot_nt(q[:, cpt * hd:(cpt + 1) * hd], k[:, cpt * hd:(cpt + 1) * hd]) for q in qs for cpt in range(2)]
        es = [jnp.exp2(s - jnp.max(s, axis=-1, keepdims=True)) for s in ss]
        inv = [1.0 / jnp.sum(e, axis=-1, keepdims=True) for e in es]
        pvs = [_dot(e.astype(BF16), v) for e in es]
        outs = []
        for r in range(len(qs)):
            o = pvs[2 * r] * inv[2 * r] - lam * (pvs[2 * r + 1] * inv[2 * r + 1])
            o = o * lax.rsqrt(jnp.mean(o * o, axis=-1, keepdims=True) + EPS) * g_ref[...] * (1.0 - lam_init)
            outs.append(o.astype(o_ref.dtype))
        return outs

    @pl.when(i * tq < seq)
    def _():
        outs = attend([q_ref[0, r:r + ATTN_SUB, :] for r in range(0, tq, ATTN_SUB)], k_ref[0], v_ref[0])
        for r, o in zip(range(0, tq, ATTN_SUB), outs):
            o_ref[0, r:r + ATTN_SUB, :] = o

    @pl.when(i * tq >= seq)
    def _():
        o_ref[0, 0:n_ctx, :] = attend([qc_ref[0]], k_ref[0, seq:t_all, :], v_ref[0, seq:t_all, :])[0]


def _diff_attention(qkv, lam_p, subln_g, seq, lam_init):
    b, t, _ = qkv.shape
    h, w = DIFF_HEADS, 2 * DIFF_HD
    n_ctx = t - seq
    tq = _pick(seq, (512, 256))
    assert tq % ATTN_SUB == 0 and 0 < n_ctx <= tq and seq % n_ctx == 0 and n_ctx % 16 == 0
    n_lat = seq // tq
    return pl.pallas_call(
        functools.partial(_attn_kernel, seq=seq, t_all=t, tq=tq, lam_init=lam_init),
        grid=(b, h, n_lat + 1),
        in_specs=[pl.BlockSpec((1, tq, w), lambda b_, h_, i: (b_, jnp.minimum(i, n_lat - 1), h_)),
                  pl.BlockSpec((1, n_ctx, w), lambda b_, h_, i: (b_, seq // n_ctx, h_)),
                  pl.BlockSpec((1, t, w), lambda b_, h_, i: (b_, 0, h + h_)),
                  pl.BlockSpec((1, t, w), lambda b_, h_, i: (b_, 0, 2 * h + h_)),
                  pl.BlockSpec((4, DIFF_HD), lambda b_, h_, i: (0, 0)),
                  pl.BlockSpec((1, w), lambda b_, h_, i: (0, 0))],
        out_specs=pl.BlockSpec((1, tq, w), lambda b_, h_, i: (b_, i, h_)),
        out_shape=jax.ShapeDtypeStruct((b, t, h * w), BF16),
        compiler_params=_cparams(("parallel", "parallel", "arbitrary")),
        name="diff_attention",
    )(qkv, qkv, qkv, qkv, lam_p, subln_g.reshape(1, w))


def _rope_tables(seq, t_all):
    rows = seq // GRID_W
    row = jnp.repeat(jnp.arange(rows), GRID_W).astype(F32)
    col = jnp.tile(jnp.arange(GRID_W), rows).astype(F32)
    n_freq = DIFF_HD // 4
    inv = ROPE_THETA ** (-jnp.arange(n_freq, dtype=F32) / n_freq)
    ang = jnp.concatenate([row[:, None] * inv, col[:, None] * inv], axis=-1)
    cos = jnp.concatenate([jnp.cos(ang), jnp.cos(ang)], axis=-1)
    sin = jnp.concatenate([-jnp.sin(ang), jnp.sin(ang)], axis=-1)
    pad = t_all - seq
    cos = jnp.concatenate([cos, jnp.ones((pad, DIFF_HD), F32)], axis=0)
    sin = jnp.concatenate([sin, jnp.zeros((pad, DIFF_HD), F32)], axis=0)
    return cos, sin


def kernel(x, c, ctx, c_ctx, w_mod, b_mod, ln1_g, ln1_b, ln2_g, ln2_b, ev_w_in, ev_conv, ev_a_log, ev_dt_bias, ev_pool_w, ev_pool_scale, ev_norm, ev_w_out, od_w_in, od_lam_q1, od_lam_k1, od_lam_q2, od_lam_k2, od_subln, od_w_out, ffn_w_up, ffn_conv, ffn_w_down):
    b, seq, d = x.shape
    t_all = seq + ctx.shape[1]
    depth = w_mod.shape[0]
    alpha = (2 * depth) ** 0.25
    pool_width = ev_pool_w.shape[1] * ev_pool_w.shape[2]
    gdn_main = pool_width + 4 * GDN_HEADS * GDN_DK
    diff_qw = DIFF_HEADS * 2 * DIFF_HD

    r_pad = -(-(b + 1) // 8) * 8
    cond = jnp.concatenate([c, c_ctx[None, :], jnp.zeros((r_pad - b - 1, d), F32)], axis=0)
    mods = _modulation(cond, w_mod, b_mod).reshape(depth, r_pad, 6, d)
    lat = mods[:, :b]
    cx = jnp.broadcast_to(mods[:, b:b + 1], lat.shape)
    zeros2 = jnp.zeros((depth, b, 2, d), F32)
    one_plus = jnp.array([0.0, 1.0, 0.0], F32)[:, None]
    mod1 = jnp.concatenate([lat[:, :, 0:3] + one_plus, cx[:, :, 0:3] + one_plus, zeros2], axis=2)
    mod2 = jnp.concatenate([lat[:, :, 3:6] + one_plus, cx[:, :, 3:6] + one_plus, zeros2], axis=2)

    cos, sin = _rope_tables(seq, t_all)
    xa = jnp.concatenate([x, ctx], axis=1)

    for l in range(depth):
        i = l // 2
        if l % 2 == 0:
            w_in = ev_w_in[i].astype(BF16)
            w_gate = jnp.pad(w_in[:, gdn_main:], ((0, 0), (0, LANES - (w_in.shape[1] - gdn_main))))
            p = _mm1(xa, mod1[l], w_in[:, :gdn_main], seq, F32)
            n_gate = 2 * GDN_HEADS
            gate_prm = jnp.zeros((3, LANES), F32)
            gate_prm = gate_prm.at[0, n_gate:2 * n_gate].set(ev_a_log[i].reshape(-1))
            gate_prm = gate_prm.at[1, n_gate:2 * n_gate].set(ev_dt_bias[i].reshape(-1))
            gate_prm = gate_prm.at[2, :n_gate].set(1.0)
            gates = _mm1(xa, mod1[l], w_gate, seq, F32, gate_act=gate_prm)
            ya = _pool_mixer(p, ev_pool_w[i].astype(BF16), ev_pool_scale[i], seq)
            o_f, o_b = _gdn_mixer(p, gates, ev_conv[i], seq)
            w_out = ev_w_out[i].astype(BF16)
            z_blk = (gdn_main - o_f.shape[2]) // o_f.shape[2]
            xa = _mm2([ya], [w_out[:pool_width], w_out[pool_width:]], xa, mod1[l], ln1_g[l], ln1_b[l], seq, alpha,
                      gdn=(o_f, o_b, p, z_blk, ev_norm[i]))
        else:
            lam_init = 0.8 - 0.6 * math.exp(-0.3 * l)
            qkv = _mm1(xa, mod1[l], od_w_in[i].astype(BF16), seq, BF16,
                       rope=(cos, sin, 2 * diff_qw, diff_qw, DIFF_HD ** -0.5 * math.log2(math.e)))
            lam_p = jnp.stack([od_lam_q1[i], od_lam_k1[i], od_lam_q2[i], od_lam_k2[i]]).astype(F32)
            y = _diff_attention(qkv, lam_p, od_subln[i], seq, lam_init)
            xa = _mm2([y], [od_w_out[i].astype(BF16)], xa, mod1[l], ln1_g[l], ln1_b[l], seq, alpha)
        xa = _ffn(xa, mod2[l], ffn_w_up[l].astype(BF16), ffn_conv[l], ffn_w_down[l].astype(BF16),
                  ln2_g[l], ln2_b[l], seq, alpha, seq if l == depth - 1 else t_all)
    return xa
```

```python
import functools
import math

import jax
import jax.numpy as jnp
from jax import lax
from jax.experimental import pallas as pl
from jax.experimental.pallas import tpu as pltpu

F32 = jnp.float32
BF16 = jnp.bfloat16

GRID_W = 64
POOL_GROUPS = 4
POOL_WINDOWS = (2, 4, 8, 16)
GDN_HEADS = 8
GDN_DK = 128
GDN_DV = 128
GDN_CHUNK = 64
DIFF_HEADS = 8
DIFF_HD = 128
ROPE_THETA = 10000.0
EPS = 1e-6

V7X_VMEM_BYTES = 64 * 1024 * 1024
VMEM_LIMIT = V7X_VMEM_BYTES - 8 * 1024 * 1024
LANES = 128
HALO = 16


def _cparams(sem):
    return pltpu.CompilerParams(dimension_semantics=sem, vmem_limit_bytes=VMEM_LIMIT)


def _sigmoid(x):
    return 1.0 / (1.0 + jnp.exp(-x))


def _silu(x):
    return x * _sigmoid(x)


def _dot(a, b):
    return jnp.dot(a, b, preferred_element_type=F32)


def _dot_nt(a, b):
    return lax.dot_general(a, b, (((1,), (1,)), ((), ())), preferred_element_type=F32)


def _dot_tn(a, b):
    return lax.dot_general(a, b, (((0,), (0,)), ((), ())), preferred_element_type=F32)


def _layer_norm_rows(r, g, b):
    mu = jnp.mean(r, axis=-1, keepdims=True)
    d = r - mu
    var = jnp.mean(d * d, axis=-1, keepdims=True)
    return d * lax.rsqrt(var + EPS) * g + b


def _pick(n, cands):
    for c in cands:
        if n % c == 0:
            return c
    raise ValueError(f"no tile in {cands} divides {n}")


def _mod_kernel(c_ref, w_ref, b_ref, o_ref):
    s = _silu(c_ref[...]).astype(BF16)
    o_ref[0] = _dot(s, w_ref[0].astype(BF16)) + b_ref[0]


def _modulation(cond, w_mod, b_mod):
    depth, d, n = w_mod.shape
    r = cond.shape[0]
    tn = _pick(n, (1024, 512, 256, 128))
    return pl.pallas_call(
        _mod_kernel,
        grid=(depth, n // tn),
        in_specs=[pl.BlockSpec((r, d), lambda l, j: (0, 0)),
                  pl.BlockSpec((1, d, tn), lambda l, j: (l, 0, j)),
                  pl.BlockSpec((1, 1, tn), lambda l, j: (l, 0, j))],
        out_specs=pl.BlockSpec((1, r, tn), lambda l, j: (l, 0, j)),
        out_shape=jax.ShapeDtypeStruct((depth, r, n), F32),
        compiler_params=_cparams(("parallel", "parallel")),
        name="modulation",
    )(cond, w_mod, b_mod.reshape(depth, 1, n))


def _per_segment(i, tm, seq, fn):
    i_mix, split = divmod(seq, tm)
    assert split % 16 == 0

    @pl.when(i < i_mix)
    def _():
        fn(slice(0, tm), 0)

    @pl.when(i > i_mix)
    def _():
        fn(slice(0, tm), 1)

    @pl.when(i == i_mix)
    def _():
        if split:
            fn(slice(0, split), 0)
        fn(slice(split, tm), 1)


def _mm1_kernel(*refs, seq, tm, rope_tiles, q_tiles, q_scale, gates):
    if rope_tiles:
        x_ref, mod_ref, w_ref, cos_ref, sin_ref, o_ref, u_scr = refs
    elif gates:
        x_ref, mod_ref, w_ref, wg_ref, ga_ref, o_ref, og_ref, u_scr = refs
    else:
        x_ref, mod_ref, w_ref, o_ref, u_scr = refs
    i = pl.program_id(1)
    j = pl.program_id(2)

    @pl.when(j == 0)
    def _():
        def modulate(rs, k):
            m = mod_ref[0]
            u_scr[rs] = (x_ref[0, rs, :] * m[3 * k + 1:3 * k + 2] + m[3 * k:3 * k + 1]).astype(BF16)

        _per_segment(i, tm, seq, modulate)
        if gates:
            raw = _dot(u_scr[...], wg_ref[...])
            ga = ga_ref[...]
            xg = raw + ga[1:2]
            softplus = jnp.maximum(xg, 0.0) + jnp.log(1.0 + jnp.exp(-jnp.abs(xg)))
            og_ref[0] = jnp.where(ga[2:3] > 0.5, _sigmoid(raw), -jnp.exp(ga[0:1]) * softplus)

    acc = _dot(u_scr[...], w_ref[...])
    if not rope_tiles:
        o_ref[0] = acc.astype(o_ref.dtype)
        return

    @pl.when(j < rope_tiles)
    def _():
        sc = jnp.where(j < q_tiles, q_scale, 1.0).astype(F32)
        cos = cos_ref[...] * sc
        sin = sin_ref[...] * sc
        for g in range(acc.shape[1] // LANES):
            blk = acc[:, g * LANES:(g + 1) * LANES]
            rot = pltpu.roll(blk, LANES // 2, axis=1)
            o_ref[0, :, g * LANES:(g + 1) * LANES] = (blk * cos + rot * sin).astype(o_ref.dtype)

    @pl.when(j >= rope_tiles)
    def _():
        o_ref[0] = acc.astype(o_ref.dtype)


def _mm1(xa, mod, w, seq, out_dtype, rope=None, gates=None):
    b, t, d = xa.shape
    n = w.shape[1]
    tm = _pick(t, (1152, 768, 512, 256, 128))
    tn = _pick(n, (1024, 512, 256, 128))
    in_specs = [pl.BlockSpec((1, tm, d), lambda b_, i, j: (b_, i, 0)),
                pl.BlockSpec((1, 8, d), lambda b_, i, j: (b_, 0, 0)),
                pl.BlockSpec((d, tn), lambda b_, i, j: (0, j))]
    args = [xa, mod, w]
    kw = dict(seq=seq, tm=tm, rope_tiles=0, q_tiles=0, q_scale=1.0, gates=gates is not None)
    out_specs = pl.BlockSpec((1, tm, tn), lambda b_, i, j: (b_, i, j))
    out_shape = jax.ShapeDtypeStruct((b, t, n), out_dtype)
    if gates is not None:
        in_specs += [pl.BlockSpec((d, LANES), lambda b_, i, j: (0, 0)), pl.BlockSpec((3, LANES), lambda b_, i, j: (0, 0))]
        args += list(gates)
        out_specs = [out_specs, pl.BlockSpec((1, tm, LANES), lambda b_, i, j: (b_, i, 0))]
        out_shape = [out_shape, jax.ShapeDtypeStruct((b, t, LANES), F32)]
    if rope is not None:
        cos, sin, rope_cols, q_cols, q_scale = rope
        assert rope_cols % tn == 0 and q_cols % tn == 0
        in_specs += [pl.BlockSpec((tm, LANES), lambda b_, i, j: (i, 0)),
                     pl.BlockSpec((tm, LANES), lambda b_, i, j: (i, 0))]
        args += [cos, sin]
        kw.update(rope_tiles=rope_cols // tn, q_tiles=q_cols // tn, q_scale=q_scale)
    return pl.pallas_call(
        functools.partial(_mm1_kernel, **kw),
        grid=(b, t // tm, n // tn),
        in_specs=in_specs,
        out_specs=out_specs,
        out_shape=out_shape,
        scratch_shapes=[pltpu.VMEM((tm, d), BF16)],
        compiler_params=_cparams(("parallel", "parallel", "arbitrary")),
        name="mod_in_proj",
    )(*args)


def _mm2_kernel(*refs, gdn, seq, tm, alpha):
    if gdn:
        ya_ref, of_ref, ob_ref, z_ref, ng_ref, wa_ref, wb_ref, x_ref, mod_ref, g_ref, b_ref, o_ref, yb_scr = refs
        for h in range(GDN_HEADS):
            sl = slice(h * GDN_DV, (h + 1) * GDN_DV)
            o = of_ref[0, :, sl] + ob_ref[0, :, sl]
            o = o * lax.rsqrt(jnp.mean(o * o, axis=-1, keepdims=True) + EPS) * ng_ref[...]
            yb_scr[:, sl] = (o * _silu(z_ref[0, :, sl])).astype(BF16)
        acc = _dot(ya_ref[0], wa_ref[...]) + _dot(yb_scr[...], wb_ref[...])
    else:
        y_ref, w_ref, x_ref, mod_ref, g_ref, b_ref, o_ref = refs
        acc = _dot(y_ref[0], w_ref[...])

    def finish(rs, k):
        r = alpha * x_ref[0, rs, :] + mod_ref[0][3 * k + 2:3 * k + 3] * acc[rs]
        o_ref[0, rs, :] = _layer_norm_rows(r, g_ref[...], b_ref[...])

    _per_segment(pl.program_id(1), tm, seq, finish)


def _mm2(ys, ws, xa, mod, ln_g, ln_b, seq, alpha, gdn=None):
    b, t, d = xa.shape
    tm = _pick(t, (384, 256, 128))
    row_blk = lambda width, cb=0: pl.BlockSpec((1, tm, width), lambda b_, i: (b_, i, cb))
    full = lambda a: pl.BlockSpec(a.shape, lambda b_, i: (0,) * a.ndim)
    in_specs = [row_blk(ys[0].shape[2])]
    args = [ys[0]]
    scratch = []
    if gdn is not None:
        o_f, o_b, p, z_blk, norm_g = gdn
        zw = o_f.shape[2]
        in_specs += [row_blk(zw), row_blk(zw), row_blk(zw, z_blk), pl.BlockSpec((1, GDN_DV), lambda b_, i: (0, 0))]
        args += [o_f, o_b, p, norm_g.reshape(1, GDN_DV)]
        scratch = [pltpu.VMEM((tm, zw), BF16)]
    in_specs += [full(w) for w in ws]
    in_specs += [row_blk(d), pl.BlockSpec((1, 8, d), lambda b_, i: (b_, 0, 0)),
                 pl.BlockSpec((1, d), lambda b_, i: (0, 0)), pl.BlockSpec((1, d), lambda b_, i: (0, 0))]
    return pl.pallas_call(
        functools.partial(_mm2_kernel, gdn=gdn is not None, seq=seq, tm=tm, alpha=alpha),
        grid=(b, t // tm),
        in_specs=in_specs,
        out_specs=row_blk(d),
        out_shape=jax.ShapeDtypeStruct((b, t, d), F32),
        scratch_shapes=scratch,
        compiler_params=_cparams(("parallel", "parallel")),
        name="out_proj_ln",
    )(*args, *ws, xa, mod, ln_g.reshape(1, d), ln_b.reshape(1, d))


def _ffn_kernel(x_ref, xp_ref, xn_ref, mod_ref, wg_ref, wu_ref, cw_ref, wd_ref, g_ref, b_ref, o_ref,
                u_scr, acc_scr, *, seq, t_all, tm, alpha):
    i = pl.program_id(1)
    j = pl.program_id(2)
    nj = pl.num_programs(2)
    row0 = i * tm

    @pl.when(j == 0)
    def _():
        m = mod_ref[0]

        def modulate(rs, k):
            u_scr[HALO + rs.start:HALO + rs.stop] = (
                x_ref[0, rs, :] * m[3 * k + 1:3 * k + 2] + m[3 * k:3 * k + 1]).astype(BF16)

        def halo(x, in_latent):
            shift = jnp.where(in_latent, m[0:1], m[3:4])
            scale1p = jnp.where(in_latent, m[1:2], m[4:5])
            return (x * scale1p + shift).astype(BF16)

        u_scr[0:HALO] = halo(xp_ref[0], row0 < seq)
        _per_segment(i, tm, seq, modulate)
        u_scr[HALO + tm:2 * HALO + tm] = halo(xn_ref[0], row0 + tm - 1 < seq)
        acc_scr[...] = jnp.zeros_like(acc_scr)

    gate_ext = _dot(u_scr[...], wg_ref[...])
    up = _dot(u_scr[HALO:HALO + tm], wu_ref[...])
    gm = gate_ext[HALO:HALO + tm]
    lrow = lax.broadcasted_iota(jnp.int32, (tm, 1), 0)
    grow = row0 + lrow
    prev = jnp.where(lrow == 0, gate_ext[HALO - 1:HALO], pltpu.roll(gm, 1, axis=0))
    nxt = jnp.where(lrow == tm - 1, gate_ext[HALO + tm:HALO + tm + 1], pltpu.roll(gm, tm - 1, axis=0))
    prev = jnp.where((grow == 0) | (grow == seq), 0.0, prev)
    nxt = jnp.where((grow == seq - 1) | (grow == t_all - 1), 0.0, nxt)
    cw = cw_ref[...]
    conv = cw[0:1] * prev + cw[1:2] * gm + cw[2:3] * nxt
    h = (_silu(conv) * up).astype(BF16)
    acc_scr[...] += _dot(h, wd_ref[...])

    @pl.when(j == nj - 1)
    def _():
        def finish(rs, k):
            r = alpha * x_ref[0, rs, :] + mod_ref[0][3 * k + 2:3 * k + 3] * acc_scr[rs]
            o_ref[0, rs, :] = _layer_norm_rows(r, g_ref[...], b_ref[...])

        _per_segment(i, tm, seq, finish)


def _ffn(xa, mod, w_up, conv_w, w_down, ln_g, ln_b, seq, alpha, out_rows):
    b, t, d = xa.shape
    dff = w_down.shape[0]
    tm = _pick(t, (768, 512, 256, 128))
    tf = _pick(dff, (512, 256, 128))
    nj = dff // tf
    hb = tm // HALO
    last_hb = t // HALO - 1
    return pl.pallas_call(
        functools.partial(_ffn_kernel, seq=seq, t_all=t, tm=tm, alpha=alpha),
        grid=(b, pl.cdiv(out_rows, tm), nj),
        in_specs=[pl.BlockSpec((1, tm, d), lambda b_, i, j: (b_, i, 0)),
                  pl.BlockSpec((1, HALO, d), lambda b_, i, j: (b_, jnp.maximum(i * hb - 1, 0), 0)),
                  pl.BlockSpec((1, HALO, d), lambda b_, i, j: (b_, jnp.minimum((i + 1) * hb, last_hb), 0)),
                  pl.BlockSpec((1, 8, d), lambda b_, i, j: (b_, 0, 0)),
                  pl.BlockSpec((d, tf), lambda b_, i, j: (0, j)),
                  pl.BlockSpec((d, tf), lambda b_, i, j: (0, nj + j)),
                  pl.BlockSpec((3, tf), lambda b_, i, j: (0, j)),
                  pl.BlockSpec((tf, d), lambda b_, i, j: (j, 0)),
                  pl.BlockSpec((1, d), lambda b_, i, j: (0, 0)),
                  pl.BlockSpec((1, d), lambda b_, i, j: (0, 0))],
        out_specs=pl.BlockSpec((1, tm, d), lambda b_, i, j: (b_, i, 0)),
        out_shape=jax.ShapeDtypeStruct((b, out_rows, d), F32),
        scratch_shapes=[pltpu.VMEM((tm + 2 * HALO, d), BF16), pltpu.VMEM((tm, d), F32)],
        compiler_params=_cparams(("parallel", "parallel", "arbitrary")),
        name="conv_glu_ffn",
    )(xa, xa, xa, mod, w_up, w_up, conv_w, w_down, ln_g.reshape(1, d), ln_b.reshape(1, d))


def _pool_kernel(a_ref, w_ref, s_ref, o_ref, *, seq, t_all):
    g = pl.program_id(1)
    a = a_ref[0].astype(F32)
    t = lax.broadcasted_iota(jnp.int32, (t_all, 1), 0)
    s0 = jnp.where(t < seq, 0, seq)
    s1 = jnp.where(t < seq, seq, t_all)
    for gi, win in enumerate(POOL_WINDOWS):
        @pl.when(g == gi)
        def _(win=win):
            tot = a
            for dlt in range(-(win // 2), win - win // 2):
                if dlt == 0:
                    continue
                shifted = pltpu.roll(a, (-dlt) % t_all, axis=0)
                ok = (t + dlt >= s0) & (t + dlt < s1)
                tot = tot + jnp.where(ok, shifted, 0.0)
            lo = jnp.maximum(t - win // 2, s0)
            hi = jnp.minimum(t + (win - win // 2), s1)
            cnt = (hi - lo).astype(F32)
            pooled = (tot * (1.0 / cnt) - a).astype(BF16)
            o_ref[0] = (_dot(pooled, w_ref[0]) * s_ref[0]).astype(o_ref.dtype)


def _pool_mixer(p, pool_w, pool_scale, seq):
    b, t, _ = p.shape
    g, gw, _ = pool_w.shape
    return pl.pallas_call(
        functools.partial(_pool_kernel, seq=seq, t_all=t),
        grid=(b, g),
        in_specs=[pl.BlockSpec((1, t, gw), lambda b_, g_: (b_, 0, g_)),
                  pl.BlockSpec((1, gw, gw), lambda b_, g_: (g_, 0, 0)),
                  pl.BlockSpec((1, 1, gw), lambda b_, g_: (g_, 0, 0))],
        out_specs=pl.BlockSpec((1, t, gw), lambda b_, g_: (b_, 0, g_)),
        out_shape=jax.ShapeDtypeStruct((b, t, g * gw), BF16),
        compiler_params=_cparams(("parallel", "parallel")),
        name="pool_mixer",
    )(p, pool_w, pool_scale.reshape(g, 1, gw))


GDN_GROUP = 12


def _gdn_pre_kernel(q_ref, k_ref, v_ref, gc_ref, gr_ref, cwq_ref, cwk_ref, cwv_ref,
                    u_ref, wq_ref, kd_ref, a_ref, eg_ref, *, seq, t_all):
    c = GDN_CHUNK
    gsz = GDN_GROUP
    rows = gsz * c
    n_groups = t_all // rows
    ri = lax.broadcasted_iota(jnp.int32, (c, c), 0)
    ci = lax.broadcasted_iota(jnp.int32, (c, c), 1)
    lower = ci <= ri
    upper = ci >= ri
    eye = (ci == ri).astype(F32)
    row = lax.broadcasted_iota(jnp.int32, (rows, 1), 0)

    def group(gi, carry):
        r0 = pl.multiple_of(gi * rows, rows)
        n0 = gi * gsz
        grow_idx = r0 + row
        no_prev = (grow_idx == 0) | (grow_idx == seq)
        no_next = (grow_idx == seq - 1) | (grow_idx == t_all - 1)
        pidx = jnp.maximum(r0 - 1, 0)
        nidx = jnp.minimum(r0 + rows, t_all - 1)

        def conv_silu(ref, cw_ref):
            xm = ref[0, pl.ds(r0, rows), :]
            xp = jnp.where(row == 0, ref[0, pl.ds(pidx, 1), :], pltpu.roll(xm, 1, axis=0))
            xn = jnp.where(row == rows - 1, ref[0, pl.ds(nidx, 1), :], pltpu.roll(xm, rows - 1, axis=0))
            xp = jnp.where(no_prev, 0.0, xp)
            xn = jnp.where(no_next, 0.0, xn)
            cw = cw_ref[...]
            return _silu(cw[0:1] * xp + cw[1:2] * xm + cw[2:3] * xn)

        def l2n(x):
            return x * lax.rsqrt(jnp.sum(x * x, axis=-1, keepdims=True) + EPS)

        q_all = l2n(conv_silu(q_ref, cwq_ref)) * (GDN_DK ** -0.5)
        k_all = l2n(conv_silu(k_ref, cwk_ref))
        v_all = conv_silu(v_ref, cwv_ref)
        gcol_all = gc_ref[0, 0, pl.ds(r0, rows), :]
        chunk = lambda x, j: x[j * c:(j + 1) * c]
        qs = [chunk(q_all, j) for j in range(gsz)]
        ks = [chunk(k_all, j) for j in range(gsz)]
        vs = [chunk(v_all, j) for j in range(gsz)]
        kbs = [k.astype(BF16) for k in ks]
        qks = [_dot_nt(q.astype(BF16), kb) for q, kb in zip(qs, kbs)]
        kks = [_dot_nt(kb, kb) for kb in kbs]
        chains = []
        for j in range(gsz):
            gcol = chunk(gcol_all, j)
            grow = gr_ref[0, 0, n0 + j]
            for d in range(2):
                beta = gcol[:, d:d + 1]
                g_c = gcol[:, 2 + d:3 + d]
                g_r = grow[2 + d:3 + d, :]
                incl = lower if d == 0 else upper
                strict = (ci < ri) if d == 0 else (ci > ri)
                inclt = upper if d == 0 else lower
                gcc = jnp.sum(jnp.where(incl, g_r, 0.0), axis=1, keepdims=True)
                gcr = jnp.sum(jnp.where(inclt, g_c, 0.0), axis=0, keepdims=True)
                decay = jnp.exp(jnp.where(incl, gcc - gcr, -jnp.inf))
                g_last = gcc[c - 1:c] if d == 0 else gcc[0:1]
                lmat = jnp.where(strict, kks[j] * beta * decay, 0.0)
                egc = jnp.exp(gcc)
                rhs = jnp.concatenate([vs[j] * beta, ks[j] * (beta * egc)], axis=1).astype(BF16)
                chains.append(dict(j=j, d=d, lmat=lmat, rhs=rhs, qg=qs[j] * egc,
                                   kd=(ks[j] * jnp.exp(g_last - gcc)).astype(BF16),
                                   amat=(qks[j] * decay).astype(BF16),
                                   eg=jnp.broadcast_to(jnp.exp(g_last), (8, LANES))))
        tmats = [eye - ch["lmat"] for ch in chains]
        pws = [ch["lmat"] for ch in chains]
        for _ in range(int(math.log2(c)) - 1):
            pws = [_dot(pw.astype(BF16), pw.astype(BF16)) for pw in pws]
            tmats = [tm + _dot(tm.astype(BF16), pw.astype(BF16)) for tm, pw in zip(tmats, pws)]
        uws = [_dot(tm.astype(BF16), ch["rhs"]) for tm, ch in zip(tmats, chains)]
        for uw, ch in zip(uws, chains):
            d, n = ch["d"], n0 + ch["j"]
            u_ref[0, d, n, 0] = uw[:, :GDN_DV]
            wq_ref[0, d, n, 0] = jnp.concatenate([uw[:, GDN_DV:], ch["qg"]], axis=0).astype(BF16)
            kd_ref[0, d, n, 0] = ch["kd"]
            a_ref[0, d, n, 0] = ch["amat"]
            eg_ref[0, d, n, 0] = ch["eg"]
        return carry

    lax.fori_loop(0, n_groups, group, 0)


GDN_SCAN_CHUNKS = 4


def _gdn_rec_kernel(uf, ub, wqf, wqb, kdf, kdb, af, ab, egf, egb, of_ref, ob_ref, s_scr):
    c = GDN_CHUNK
    ncs = GDN_SCAN_CHUNKS

    @pl.when(pl.program_id(1) == 0)
    def _():
        s_scr[...] = jnp.zeros_like(s_scr)

    streams = ((uf, wqf, kdf, af, egf, of_ref), (ub, wqb, kdb, ab, egb, ob_ref))
    ids = [(d, h) for d in range(2) for h in range(GDN_HEADS)]
    for t in range(ncs):
        cc = (t, ncs - 1 - t)
        sts = [s_scr[d, h] for d, h in ids]
        wqs = [_dot(streams[d][1][0, 0, cc[d], h], st.astype(BF16)) for (d, h), st in zip(ids, sts)]
        vbs = [(streams[d][0][0, 0, cc[d], h] - wq[:c]).astype(BF16) for (d, h), wq in zip(ids, wqs)]
        for (d, h), wq, vb in zip(ids, wqs, vbs):
            streams[d][5][0, cc[d] * c:(cc[d] + 1) * c, h * GDN_DV:(h + 1) * GDN_DV] = (
                wq[c:] + _dot(streams[d][3][0, 0, cc[d], h], vb))
        for (d, h), st, vb in zip(ids, sts, vbs):
            s_scr[d, h] = st * streams[d][4][0, 0, cc[d], h][0:1] + _dot_tn(streams[d][2][0, 0, cc[d], h], vb)


def _gdn_mixer(p, gates, conv_w, seq):
    b, t, _ = p.shape
    h, c = GDN_HEADS, GDN_CHUNK
    nc = t // c
    seq_c = seq // c
    ctx_c = nc - seq_c
    assert seq % c == 0 and t % (GDN_GROUP * c) == 0
    pool_blocks = (p.shape[2] - 4 * h * GDN_DK) // LANES
    gates = gates[..., :4 * h].reshape(b, t, 4, h)
    g_col = gates.transpose(0, 3, 1, 2)
    g_row = gates.reshape(b, nc, c, 4, h).transpose(0, 4, 1, 3, 2)
    col = lambda off: pl.BlockSpec((1, t, LANES), lambda b_, h_: (b_, 0, pool_blocks + off * h + h_))
    cw = lambda off: pl.BlockSpec((3, LANES), lambda b_, h_: (0, off * h + h_))
    pre_shapes = [((c, GDN_DV), F32), ((2 * c, GDN_DK), BF16), ((c, GDN_DK), BF16), ((c, c), BF16), ((8, LANES), F32)]
    pre = pl.pallas_call(
        functools.partial(_gdn_pre_kernel, seq=seq, t_all=t),
        grid=(b, h),
        in_specs=[col(0), col(1), col(2),
                  pl.BlockSpec((1, 1, t, 4), lambda b_, h_: (b_, h_, 0, 0)),
                  pl.BlockSpec((1, 1, nc, 4, c), lambda b_, h_: (b_, h_, 0, 0, 0)),
                  cw(0), cw(1), cw(2)],
        out_specs=[pl.BlockSpec((1, 2, nc, 1) + s, lambda b_, h_: (b_, 0, 0, h_, 0, 0)) for s, _ in pre_shapes],
        out_shape=[jax.ShapeDtypeStruct((b, 2, nc, h) + s, dt) for s, dt in pre_shapes],
        compiler_params=_cparams(("parallel", "parallel")),
        name="gated_delta_pre",
    )(p, p, p, g_col, g_row, conv_w, conv_w, conv_w)

    ncs = GDN_SCAN_CHUNKS
    assert seq_c % ncs == 0 and ctx_c % ncs == 0
    nblk, seq_b, ctx_b = nc // ncs, seq_c // ncs, ctx_c // ncs
    fwd_blk = lambda s: jnp.where(s < ctx_b, seq_b + s, s - ctx_b)
    bwd_blk = lambda s: nblk - 1 - s
    in_specs, args = [], []
    for arr, (s_, _) in zip(pre, pre_shapes):
        for d, blk in enumerate((fwd_blk, bwd_blk)):
            in_specs.append(pl.BlockSpec((1, 1, ncs, h) + s_, lambda b_, s, d=d, blk=blk: (b_, d, blk(s), 0, 0, 0)))
            args.append(arr)
    return pl.pallas_call(
        _gdn_rec_kernel,
        grid=(b, nblk),
        in_specs=in_specs,
        out_specs=[pl.BlockSpec((1, ncs * c, h * GDN_DV), lambda b_, s: (b_, fwd_blk(s), 0)),
                   pl.BlockSpec((1, ncs * c, h * GDN_DV), lambda b_, s: (b_, bwd_blk(s), 0))],
        out_shape=[jax.ShapeDtypeStruct((b, t, h * GDN_DV), F32)] * 2,
        scratch_shapes=[pltpu.VMEM((2, h, GDN_DK, GDN_DV), F32)],
        compiler_params=_cparams(("parallel", "arbitrary")),
        name="gated_delta_scan",
    )(*args)


ATTN_SUB = 256


def _attn_kernel(q_ref, qc_ref, k_ref, v_ref, lam_ref, g_ref, o_ref, *, seq, t_all, tq, lam_init):
    i = pl.program_id(2)
    hd = DIFF_HD
    n_ctx = t_all - seq
    lp = lam_ref[...]
    lam = (jnp.exp(jnp.sum(lp[0:1] * lp[1:2], axis=-1, keepdims=True))
           - jnp.exp(jnp.sum(lp[2:3] * lp[3:4], axis=-1, keepdims=True)) + lam_init)

    def attend(qs, k, v):
        ss = [_dot_nt(q[:, cpt * hd:(cpt + 1) * hd], k[:, cpt * hd:(cpt + 1) * hd]) for q in qs for cpt in range(2)]
        es = [jnp.exp2(s - jnp.max(s, axis=-1, keepdims=True)) for s in ss]
        inv = [1.0 / jnp.sum(e, axis=-1, keepdims=True) for e in es]
        pvs = [_dot(e.astype(BF16), v) for e in es]
        outs = []
        for r in range(len(qs)):
            o = pvs[2 * r] * inv[2 * r] - lam * (pvs[2 * r + 1] * inv[2 * r + 1])
            o = o * lax.rsqrt(jnp.mean(o * o, axis=-1, keepdims=True) + EPS) * g_ref[...] * (1.0 - lam_init)
            outs.append(o.astype(o_ref.dtype))
        return outs

    @pl.when(i * tq < seq)
    def _():
        outs = attend([q_ref[0, r:r + ATTN_SUB, :] for r in range(0, tq, ATTN_SUB)], k_ref[0], v_ref[0])
        for r, o in zip(range(0, tq, ATTN_SUB), outs):
            o_ref[0, r:r + ATTN_SUB, :] = o

    @pl.when(i * tq >= seq)
    def _():
        o_ref[0, 0:n_ctx, :] = attend([qc_ref[0]], k_ref[0, seq:t_all, :], v_ref[0, seq:t_all, :])[0]


def _diff_attention(qkv, lam_p, subln_g, seq, lam_init):
    b, t, _ = qkv.shape
    h, w = DIFF_HEADS, 2 * DIFF_HD
    n_ctx = t - seq
    tq = _pick(seq, (512, 256))
    assert tq % ATTN_SUB == 0 and 0 < n_ctx <= tq and seq % n_ctx == 0 and n_ctx % 16 == 0
    n_lat = seq // tq
    return pl.pallas_call(
        functools.partial(_attn_kernel, seq=seq, t_all=t, tq=tq, lam_init=lam_init),
        grid=(b, h, n_lat + 1),
        in_specs=[pl.BlockSpec((1, tq, w), lambda b_, h_, i: (b_, jnp.minimum(i, n_lat - 1), h_)),
                  pl.BlockSpec((1, n_ctx, w), lambda b_, h_, i: (b_, seq // n_ctx, h_)),
                  pl.BlockSpec((1, t, w), lambda b_, h_, i: (b_, 0, h + h_)),
                  pl.BlockSpec((1, t, w), lambda b_, h_, i: (b_, 0, 2 * h + h_)),
                  pl.BlockSpec((4, DIFF_HD), lambda b_, h_, i: (0, 0)),
                  pl.BlockSpec((1, w), lambda b_, h_, i: (0, 0))],
        out_specs=pl.BlockSpec((1, tq, w), lambda b_, h_, i: (b_, i, h_)),
        out_shape=jax.ShapeDtypeStruct((b, t, h * w), BF16),
        compiler_params=_cparams(("parallel", "parallel", "arbitrary")),
        name="diff_attention",
    )(qkv, qkv, qkv, qkv, lam_p, subln_g.reshape(1, w))


def _rope_tables(seq, t_all):
    rows = seq // GRID_W
    row = jnp.repeat(jnp.arange(rows), GRID_W).astype(F32)
    col = jnp.tile(jnp.arange(GRID_W), rows).astype(F32)
    n_freq = DIFF_HD // 4
    inv = ROPE_THETA ** (-jnp.arange(n_freq, dtype=F32) / n_freq)
    ang = jnp.concatenate([row[:, None] * inv, col[:, None] * inv], axis=-1)
    cos = jnp.concatenate([jnp.cos(ang), jnp.cos(ang)], axis=-1)
    sin = jnp.concatenate([-jnp.sin(ang), jnp.sin(ang)], axis=-1)
    pad = t_all - seq
    cos = jnp.concatenate([cos, jnp.ones((pad, DIFF_HD), F32)], axis=0)
    sin = jnp.concatenate([sin, jnp.zeros((pad, DIFF_HD), F32)], axis=0)
    return cos, sin


def kernel(x, c, ctx, c_ctx, w_mod, b_mod, ln1_g, ln1_b, ln2_g, ln2_b, ev_w_in, ev_conv, ev_a_log, ev_dt_bias, ev_pool_w, ev_pool_scale, ev_norm, ev_w_out, od_w_in, od_lam_q1, od_lam_k1, od_lam_q2, od_lam_k2, od_subln, od_w_out, ffn_w_up, ffn_conv, ffn_w_down):
    b, seq, d = x.shape
    t_all = seq + ctx.shape[1]
    depth = w_mod.shape[0]
    alpha = (2 * depth) ** 0.25
    pool_width = ev_pool_w.shape[1] * ev_pool_w.shape[2]
    gdn_main = pool_width + 4 * GDN_HEADS * GDN_DK
    diff_qw = DIFF_HEADS * 2 * DIFF_HD

    r_pad = -(-(b + 1) // 8) * 8
    cond = jnp.concatenate([c, c_ctx[None, :], jnp.zeros((r_pad - b - 1, d), F32)], axis=0)
    mods = _modulation(cond, w_mod, b_mod).reshape(depth, r_pad, 6, d)
    lat = mods[:, :b]
    cx = jnp.broadcast_to(mods[:, b:b + 1], lat.shape)
    zeros2 = jnp.zeros((depth, b, 2, d), F32)
    one_plus = jnp.array([0.0, 1.0, 0.0], F32)[:, None]
    mod1 = jnp.concatenate([lat[:, :, 0:3] + one_plus, cx[:, :, 0:3] + one_plus, zeros2], axis=2)
    mod2 = jnp.concatenate([lat[:, :, 3:6] + one_plus, cx[:, :, 3:6] + one_plus, zeros2], axis=2)

    cos, sin = _rope_tables(seq, t_all)
    xa = jnp.concatenate([x, ctx], axis=1)

    for l in range(depth):
        i = l // 2
        if l % 2 == 0:
            w_in = ev_w_in[i].astype(BF16)
            w_gate = jnp.pad(w_in[:, gdn_main:], ((0, 0), (0, LANES - (w_in.shape[1] - gdn_main))))
            n_gate = 2 * GDN_HEADS
            gate_prm = jnp.zeros((3, LANES), F32)
            gate_prm = gate_prm.at[0, n_gate:2 * n_gate].set(ev_a_log[i].reshape(-1))
            gate_prm = gate_prm.at[1, n_gate:2 * n_gate].set(ev_dt_bias[i].reshape(-1))
            gate_prm = gate_prm.at[2, :n_gate].set(1.0)
            p, gates = _mm1(xa, mod1[l], w_in[:, :gdn_main], seq, F32, gates=(w_gate, gate_prm))
            ya = _pool_mixer(p, ev_pool_w[i].astype(BF16), ev_pool_scale[i], seq)
            o_f, o_b = _gdn_mixer(p, gates, ev_conv[i], seq)
            w_out = ev_w_out[i].astype(BF16)
            z_blk = (gdn_main - o_f.shape[2]) // o_f.shape[2]
            xa = _mm2([ya], [w_out[:pool_width], w_out[pool_width:]], xa, mod1[l], ln1_g[l], ln1_b[l], seq, alpha,
                      gdn=(o_f, o_b, p, z_blk, ev_norm[i]))
        else:
            lam_init = 0.8 - 0.6 * math.exp(-0.3 * l)
            qkv = _mm1(xa, mod1[l], od_w_in[i].astype(BF16), seq, BF16,
                       rope=(cos, sin, 2 * diff_qw, diff_qw, DIFF_HD ** -0.5 * math.log2(math.e)))
            lam_p = jnp.stack([od_lam_q1[i], od_lam_k1[i], od_lam_q2[i], od_lam_k2[i]]).astype(F32)
            y = _diff_attention(qkv, lam_p, od_subln[i], seq, lam_init)
            xa = _mm2([y], [od_w_out[i].astype(BF16)], xa, mod1[l], ln1_g[l], ln1_b[l], seq, alpha)
        xa = _ffn(xa, mod2[l], ffn_w_up[l].astype(BF16), ffn_conv[l], ffn_w_down[l].astype(BF16),
                  ln2_g[l], ln2_b[l], seq, alpha, seq if l == depth - 1 else t_all)
    return xa
```

```python
import functools
import math

import jax
import jax.numpy as jnp
from jax import lax
from jax.experimental import pallas as pl
from jax.experimental.pallas import tpu as pltpu

F32 = jnp.float32
BF16 = jnp.bfloat16

GRID_W = 64
POOL_GROUPS = 4
POOL_WINDOWS = (2, 4, 8, 16)
GDN_HEADS = 8
GDN_DK = 128
GDN_DV = 128
GDN_CHUNK = 64
DIFF_HEADS = 8
DIFF_HD = 128
ROPE_THETA = 10000.0
EPS = 1e-6

V7X_VMEM_BYTES = 64 * 1024 * 1024
VMEM_LIMIT = V7X_VMEM_BYTES - 8 * 1024 * 1024
LANES = 128
HALO = 16


def _cparams(sem):
    return pltpu.CompilerParams(dimension_semantics=sem, vmem_limit_bytes=VMEM_LIMIT)


def _sigmoid(x):
    return 1.0 / (1.0 + jnp.exp(-x))


def _silu(x):
    return x * _sigmoid(x)


def _dot(a, b):
    return jnp.dot(a, b, preferred_element_type=F32)


def _dot_nt(a, b):
    return lax.dot_general(a, b, (((1,), (1,)), ((), ())), preferred_element_type=F32)


def _dot_tn(a, b):
    return lax.dot_general(a, b, (((0,), (0,)), ((), ())), preferred_element_type=F32)


def _layer_norm_rows(r, g, b):
    mu = jnp.mean(r, axis=-1, keepdims=True)
    d = r - mu
    var = jnp.mean(d * d, axis=-1, keepdims=True)
    return d * lax.rsqrt(var + EPS) * g + b


def _pick(n, cands):
    for c in cands:
        if n % c == 0:
            return c
    raise ValueError(f"no tile in {cands} divides {n}")


def _mod_kernel(c_ref, w_ref, b_ref, o_ref):
    s = _silu(c_ref[...]).astype(BF16)
    o_ref[0] = _dot(s, w_ref[0].astype(BF16)) + b_ref[0]


def _modulation(cond, w_mod, b_mod):
    depth, d, n = w_mod.shape
    r = cond.shape[0]
    tn = _pick(n, (1024, 512, 256, 128))
    return pl.pallas_call(
        _mod_kernel,
        grid=(depth, n // tn),
        in_specs=[pl.BlockSpec((r, d), lambda l, j: (0, 0)),
                  pl.BlockSpec((1, d, tn), lambda l, j: (l, 0, j)),
                  pl.BlockSpec((1, 1, tn), lambda l, j: (l, 0, j))],
        out_specs=pl.BlockSpec((1, r, tn), lambda l, j: (l, 0, j)),
        out_shape=jax.ShapeDtypeStruct((depth, r, n), F32),
        compiler_params=_cparams(("parallel", "parallel")),
        name="modulation",
    )(cond, w_mod, b_mod.reshape(depth, 1, n))


def _per_segment(i, tm, seq, fn):
    i_mix, split = divmod(seq, tm)
    assert split % 16 == 0

    @pl.when(i < i_mix)
    def _():
        fn(slice(0, tm), 0)

    @pl.when(i > i_mix)
    def _():
        fn(slice(0, tm), 1)

    @pl.when(i == i_mix)
    def _():
        if split:
            fn(slice(0, split), 0)
        fn(slice(split, tm), 1)


def _mm1_kernel(*refs, seq, tm, rope_tiles, q_tiles, q_scale, gates):
    if rope_tiles:
        x_ref, mod_ref, w_ref, cos_ref, sin_ref, o_ref, u_scr = refs
    elif gates:
        x_ref, mod_ref, w_ref, wg_ref, ga_ref, o_ref, og_ref, u_scr = refs
    else:
        x_ref, mod_ref, w_ref, o_ref, u_scr = refs
    i = pl.program_id(1)
    j = pl.program_id(2)

    @pl.when(j == 0)
    def _():
        def modulate(rs, k):
            m = mod_ref[0]
            u_scr[rs] = (x_ref[0, rs, :] * m[3 * k + 1:3 * k + 2] + m[3 * k:3 * k + 1]).astype(BF16)

        _per_segment(i, tm, seq, modulate)
        if gates:
            raw = _dot(u_scr[...], wg_ref[...])
            ga = ga_ref[...]
            xg = raw + ga[1:2]
            softplus = jnp.maximum(xg, 0.0) + jnp.log(1.0 + jnp.exp(-jnp.abs(xg)))
            og_ref[0] = jnp.where(ga[2:3] > 0.5, _sigmoid(raw), -jnp.exp(ga[0:1]) * softplus)

    acc = _dot(u_scr[...], w_ref[...])
    if not rope_tiles:
        o_ref[0] = acc.astype(o_ref.dtype)
        return

    @pl.when(j < rope_tiles)
    def _():
        sc = jnp.where(j < q_tiles, q_scale, 1.0).astype(F32)
        cos = cos_ref[...] * sc
        sin = sin_ref[...] * sc
        for g in range(acc.shape[1] // LANES):
            blk = acc[:, g * LANES:(g + 1) * LANES]
            rot = pltpu.roll(blk, LANES // 2, axis=1)
            o_ref[0, :, g * LANES:(g + 1) * LANES] = (blk * cos + rot * sin).astype(o_ref.dtype)

    @pl.when(j >= rope_tiles)
    def _():
        o_ref[0] = acc.astype(o_ref.dtype)


def _mm1(xa, mod, w, seq, out_dtype, rope=None, gates=None):
    b, t, d = xa.shape
    n = w.shape[1]
    tm = _pick(t, (1152, 768, 512, 256, 128))
    tn = _pick(n, (1024, 512, 256, 128))
    in_specs = [pl.BlockSpec((1, tm, d), lambda b_, i, j: (b_, i, 0)),
                pl.BlockSpec((1, 8, d), lambda b_, i, j: (b_, 0, 0)),
                pl.BlockSpec((d, tn), lambda b_, i, j: (0, j))]
    args = [xa, mod, w]
    kw = dict(seq=seq, tm=tm, rope_tiles=0, q_tiles=0, q_scale=1.0, gates=gates is not None)
    out_specs = pl.BlockSpec((1, tm, tn), lambda b_, i, j: (b_, i, j))
    out_shape = jax.ShapeDtypeStruct((b, t, n), out_dtype)
    if gates is not None:
        in_specs += [pl.BlockSpec((d, LANES), lambda b_, i, j: (0, 0)), pl.BlockSpec((3, LANES), lambda b_, i, j: (0, 0))]
        args += list(gates)
        out_specs = [out_specs, pl.BlockSpec((1, tm, LANES), lambda b_, i, j: (b_, i, 0))]
        out_shape = [out_shape, jax.ShapeDtypeStruct((b, t, LANES), F32)]
    if rope is not None:
        cos, sin, rope_cols, q_cols, q_scale = rope
        assert rope_cols % tn == 0 and q_cols % tn == 0
        in_specs += [pl.BlockSpec((tm, LANES), lambda b_, i, j: (i, 0)),
                     pl.BlockSpec((tm, LANES), lambda b_, i, j: (i, 0))]
        args += [cos, sin]
        kw.update(rope_tiles=rope_cols // tn, q_tiles=q_cols // tn, q_scale=q_scale)
    return pl.pallas_call(
        functools.partial(_mm1_kernel, **kw),
        grid=(b, t // tm, n // tn),
        in_specs=in_specs,
        out_specs=out_specs,
        out_shape=out_shape,
        scratch_shapes=[pltpu.VMEM((tm, d), BF16)],
        compiler_params=_cparams(("parallel", "parallel", "arbitrary")),
        name="mod_in_proj",
    )(*args)


def _mm2_kernel(*refs, gdn, seq, tm, alpha):
    if gdn:
        ya_ref, of_ref, ob_ref, z_ref, ng_ref, wa_ref, wb_ref, x_ref, mod_ref, g_ref, b_ref, o_ref, yb_scr = refs
        for h in range(GDN_HEADS):
            sl = slice(h * GDN_DV, (h + 1) * GDN_DV)
            o = of_ref[0, :, sl] + ob_ref[0, :, sl]
            o = o * lax.rsqrt(jnp.mean(o * o, axis=-1, keepdims=True) + EPS) * ng_ref[...]
            yb_scr[:, sl] = (o * _silu(z_ref[0, :, sl])).astype(BF16)
        acc = _dot(ya_ref[0], wa_ref[...]) + _dot(yb_scr[...], wb_ref[...])
    else:
        y_ref, w_ref, x_ref, mod_ref, g_ref, b_ref, o_ref = refs
        acc = _dot(y_ref[0], w_ref[...])

    def finish(rs, k):
        r = alpha * x_ref[0, rs, :] + mod_ref[0][3 * k + 2:3 * k + 3] * acc[rs]
        o_ref[0, rs, :] = _layer_norm_rows(r, g_ref[...], b_ref[...])

    _per_segment(pl.program_id(1), tm, seq, finish)


def _mm2(ys, ws, xa, mod, ln_g, ln_b, seq, alpha, out_rows, gdn=None):
    b, t, d = xa.shape
    tm = _pick(out_rows, ((512,) if gdn is None else ()) + (384, 256, 128))
    row_blk = lambda width, cb=0: pl.BlockSpec((1, tm, width), lambda b_, i: (b_, i, cb))
    full = lambda a: pl.BlockSpec(a.shape, lambda b_, i: (0,) * a.ndim)
    in_specs = [row_blk(ys[0].shape[2])]
    args = [ys[0]]
    scratch = []
    if gdn is not None:
        o_f, o_b, p, z_blk, norm_g = gdn
        zw = o_f.shape[2]
        in_specs += [row_blk(zw), row_blk(zw), row_blk(zw, z_blk), pl.BlockSpec((1, GDN_DV), lambda b_, i: (0, 0))]
        args += [o_f, o_b, p, norm_g.reshape(1, GDN_DV)]
        scratch = [pltpu.VMEM((tm, zw), BF16)]
    in_specs += [full(w) for w in ws]
    in_specs += [row_blk(d), pl.BlockSpec((1, 8, d), lambda b_, i: (b_, 0, 0)),
                 pl.BlockSpec((1, d), lambda b_, i: (0, 0)), pl.BlockSpec((1, d), lambda b_, i: (0, 0))]
    return pl.pallas_call(
        functools.partial(_mm2_kernel, gdn=gdn is not None, seq=seq, tm=tm, alpha=alpha),
        grid=(b, out_rows // tm),
        in_specs=in_specs,
        out_specs=row_blk(d),
        out_shape=jax.ShapeDtypeStruct((b, out_rows, d), F32),
        scratch_shapes=scratch,
        compiler_params=_cparams(("parallel", "parallel")),
        name="out_proj_ln",
    )(*args, *ws, xa, mod, ln_g.reshape(1, d), ln_b.reshape(1, d))


def _ffn_kernel(x_ref, xp_ref, xn_ref, mod_ref, wg_ref, wu_ref, cw_ref, wd_ref, g_ref, b_ref, o_ref,
                u_scr, acc_scr, *, seq, t_all, tm, alpha):
    i = pl.program_id(1)
    j = pl.program_id(2)
    nj = pl.num_programs(2)
    row0 = i * tm

    @pl.when(j == 0)
    def _():
        m = mod_ref[0]

        def modulate(rs, k):
            u_scr[HALO + rs.start:HALO + rs.stop] = (
                x_ref[0, rs, :] * m[3 * k + 1:3 * k + 2] + m[3 * k:3 * k + 1]).astype(BF16)

        def halo(x, in_latent):
            shift = jnp.where(in_latent, m[0:1], m[3:4])
            scale1p = jnp.where(in_latent, m[1:2], m[4:5])
            return (x * scale1p + shift).astype(BF16)

        u_scr[0:HALO] = halo(xp_ref[0], row0 < seq)
        _per_segment(i, tm, seq, modulate)
        u_scr[HALO + tm:2 * HALO + tm] = halo(xn_ref[0], row0 + tm - 1 < seq)
        acc_scr[...] = jnp.zeros_like(acc_scr)

    gate_ext = _dot(u_scr[...], wg_ref[...])
    up = _dot(u_scr[HALO:HALO + tm], wu_ref[...])
    gm = gate_ext[HALO:HALO + tm]
    lrow = lax.broadcasted_iota(jnp.int32, (tm, 1), 0)
    grow = row0 + lrow
    prev = jnp.where(lrow == 0, gate_ext[HALO - 1:HALO], pltpu.roll(gm, 1, axis=0))
    nxt = jnp.where(lrow == tm - 1, gate_ext[HALO + tm:HALO + tm + 1], pltpu.roll(gm, tm - 1, axis=0))
    prev = jnp.where((grow == 0) | (grow == seq), 0.0, prev)
    nxt = jnp.where((grow == seq - 1) | (grow == t_all - 1), 0.0, nxt)
    cw = cw_ref[...]
    conv = cw[0:1] * prev + cw[1:2] * gm + cw[2:3] * nxt
    h = (_silu(conv) * up).astype(BF16)
    acc_scr[...] += _dot(h, wd_ref[...])

    @pl.when(j == nj - 1)
    def _():
        def finish(rs, k):
            r = alpha * x_ref[0, rs, :] + mod_ref[0][3 * k + 2:3 * k + 3] * acc_scr[rs]
            o_ref[0, rs, :] = _layer_norm_rows(r, g_ref[...], b_ref[...])

        _per_segment(i, tm, seq, finish)


def _ffn(xa, mod, w_up, conv_w, w_down, ln_g, ln_b, seq, alpha, out_rows):
    b, t, d = xa.shape
    dff = w_down.shape[0]
    tm = _pick(out_rows, (768, 512, 256, 128))
    tf = _pick(dff, (512, 256, 128))
    nj = dff // tf
    hb = tm // HALO
    last_hb = t // HALO - 1
    return pl.pallas_call(
        functools.partial(_ffn_kernel, seq=seq, t_all=t, tm=tm, alpha=alpha),
        grid=(b, pl.cdiv(out_rows, tm), nj),
        in_specs=[pl.BlockSpec((1, tm, d), lambda b_, i, j: (b_, i, 0)),
                  pl.BlockSpec((1, HALO, d), lambda b_, i, j: (b_, jnp.maximum(i * hb - 1, 0), 0)),
                  pl.BlockSpec((1, HALO, d), lambda b_, i, j: (b_, jnp.minimum((i + 1) * hb, last_hb), 0)),
                  pl.BlockSpec((1, 8, d), lambda b_, i, j: (b_, 0, 0)),
                  pl.BlockSpec((d, tf), lambda b_, i, j: (0, j)),
                  pl.BlockSpec((d, tf), lambda b_, i, j: (0, nj + j)),
                  pl.BlockSpec((3, tf), lambda b_, i, j: (0, j)),
                  pl.BlockSpec((tf, d), lambda b_, i, j: (j, 0)),
                  pl.BlockSpec((1, d), lambda b_, i, j: (0, 0)),
                  pl.BlockSpec((1, d), lambda b_, i, j: (0, 0))],
        out_specs=pl.BlockSpec((1, tm, d), lambda b_, i, j: (b_, i, 0)),
        out_shape=jax.ShapeDtypeStruct((b, out_rows, d), F32),
        scratch_shapes=[pltpu.VMEM((tm + 2 * HALO, d), BF16), pltpu.VMEM((tm, d), F32)],
        compiler_params=_cparams(("parallel", "parallel", "arbitrary")),
        name="conv_glu_ffn",
    )(xa, xa, xa, mod, w_up, w_up, conv_w, w_down, ln_g.reshape(1, d), ln_b.reshape(1, d))


def _pool_kernel(a_ref, w_ref, s_ref, o_ref, *, seq, t_all):
    g = pl.program_id(1)
    a = a_ref[0].astype(F32)
    t = lax.broadcasted_iota(jnp.int32, (t_all, 1), 0)
    s0 = jnp.where(t < seq, 0, seq)
    s1 = jnp.where(t < seq, seq, t_all)
    for gi, win in enumerate(POOL_WINDOWS):
        @pl.when(g == gi)
        def _(win=win):
            tot = a
            for dlt in range(-(win // 2), win - win // 2):
                if dlt == 0:
                    continue
                shifted = pltpu.roll(a, (-dlt) % t_all, axis=0)
                ok = (t + dlt >= s0) & (t + dlt < s1)
                tot = tot + jnp.where(ok, shifted, 0.0)
            lo = jnp.maximum(t - win // 2, s0)
            hi = jnp.minimum(t + (win - win // 2), s1)
            cnt = (hi - lo).astype(F32)
            pooled = (tot * (1.0 / cnt) - a).astype(BF16)
            o_ref[0] = (_dot(pooled, w_ref[0]) * s_ref[0]).astype(o_ref.dtype)


def _pool_mixer(p, pool_w, pool_scale, seq):
    b, t, _ = p.shape
    g, gw, _ = pool_w.shape
    return pl.pallas_call(
        functools.partial(_pool_kernel, seq=seq, t_all=t),
        grid=(b, g),
        in_specs=[pl.BlockSpec((1, t, gw), lambda b_, g_: (b_, 0, g_)),
                  pl.BlockSpec((1, gw, gw), lambda b_, g_: (g_, 0, 0)),
                  pl.BlockSpec((1, 1, gw), lambda b_, g_: (g_, 0, 0))],
        out_specs=pl.BlockSpec((1, t, gw), lambda b_, g_: (b_, 0, g_)),
        out_shape=jax.ShapeDtypeStruct((b, t, g * gw), BF16),
        compiler_params=_cparams(("parallel", "parallel")),
        name="pool_mixer",
    )(p, pool_w, pool_scale.reshape(g, 1, gw))


GDN_GROUP = 12


def _gdn_pre_kernel(q_ref, k_ref, v_ref, gc_ref, gr_ref, cwq_ref, cwk_ref, cwv_ref,
                    u_ref, wq_ref, kd_ref, a_ref, eg_ref, *, seq, t_all):
    c = GDN_CHUNK
    gsz = GDN_GROUP
    rows = gsz * c
    n_groups = t_all // rows
    ri = lax.broadcasted_iota(jnp.int32, (c, c), 0)
    ci = lax.broadcasted_iota(jnp.int32, (c, c), 1)
    lower = ci <= ri
    upper = ci >= ri
    eye = (ci == ri).astype(F32)
    row = lax.broadcasted_iota(jnp.int32, (rows, 1), 0)

    def group(gi, carry):
        r0 = pl.multiple_of(gi * rows, rows)
        n0 = gi * gsz
        grow_idx = r0 + row
        no_prev = (grow_idx == 0) | (grow_idx == seq)
        no_next = (grow_idx == seq - 1) | (grow_idx == t_all - 1)
        pidx = jnp.maximum(r0 - 1, 0)
        nidx = jnp.minimum(r0 + rows, t_all - 1)

        def conv_silu(ref, cw_ref):
            xm = ref[0, pl.ds(r0, rows), :]
            xp = jnp.where(row == 0, ref[0, pl.ds(pidx, 1), :], pltpu.roll(xm, 1, axis=0))
            xn = jnp.where(row == rows - 1, ref[0, pl.ds(nidx, 1), :], pltpu.roll(xm, rows - 1, axis=0))
            xp = jnp.where(no_prev, 0.0, xp)
            xn = jnp.where(no_next, 0.0, xn)
            cw = cw_ref[...]
            return _silu(cw[0:1] * xp + cw[1:2] * xm + cw[2:3] * xn)

        def l2n(x):
            return x * lax.rsqrt(jnp.sum(x * x, axis=-1, keepdims=True) + EPS)

        q_all = l2n(conv_silu(q_ref, cwq_ref)) * (GDN_DK ** -0.5)
        k_all = l2n(conv_silu(k_ref, cwk_ref))
        v_all = conv_silu(v_ref, cwv_ref)
        gcol_all = gc_ref[0, 0, pl.ds(r0, rows), :]
        chunk = lambda x, j: x[j * c:(j + 1) * c]
        qs = [chunk(q_all, j) for j in range(gsz)]
        ks = [chunk(k_all, j) for j in range(gsz)]
        vs = [chunk(v_all, j) for j in range(gsz)]
        kbs = [k.astype(BF16) for k in ks]
        qks = [_dot_nt(q.astype(BF16), kb) for q, kb in zip(qs, kbs)]
        kks = [_dot_nt(kb, kb) for kb in kbs]
        chains = []
        for j in range(gsz):
            gcol = chunk(gcol_all, j)
            grow = gr_ref[0, 0, n0 + j]
            for d in range(2):
                beta = gcol[:, d:d + 1]
                g_c = gcol[:, 2 + d:3 + d]
                g_r = grow[2 + d:3 + d, :]
                incl = lower if d == 0 else upper
                strict = (ci < ri) if d == 0 else (ci > ri)
                inclt = upper if d == 0 else lower
                gcc = jnp.sum(jnp.where(incl, g_r, 0.0), axis=1, keepdims=True)
                gcr = jnp.sum(jnp.where(inclt, g_c, 0.0), axis=0, keepdims=True)
                decay = jnp.exp(jnp.where(incl, gcc - gcr, -jnp.inf))
                g_last = gcc[c - 1:c] if d == 0 else gcc[0:1]
                lmat = jnp.where(strict, kks[j] * beta * decay, 0.0)
                egc = jnp.exp(gcc)
                rhs = jnp.concatenate([vs[j] * beta, ks[j] * (beta * egc)], axis=1).astype(BF16)
                chains.append(dict(j=j, d=d, lmat=lmat, rhs=rhs, qg=qs[j] * egc,
                                   kd=(ks[j] * jnp.exp(g_last - gcc)).astype(BF16),
                                   amat=(qks[j] * decay).astype(BF16),
                                   eg=jnp.broadcast_to(jnp.exp(g_last), (8, LANES))))
        tmats = [eye - ch["lmat"] for ch in chains]
        pws = [ch["lmat"] for ch in chains]
        for _ in range(int(math.log2(c)) - 1):
            pws = [_dot(pw.astype(BF16), pw.astype(BF16)) for pw in pws]
            tmats = [tm + _dot(tm.astype(BF16), pw.astype(BF16)) for tm, pw in zip(tmats, pws)]
        uws = [_dot(tm.astype(BF16), ch["rhs"]) for tm, ch in zip(tmats, chains)]
        for uw, ch in zip(uws, chains):
            d, n = ch["d"], n0 + ch["j"]
            u_ref[0, d, n, 0] = uw[:, :GDN_DV]
            wq_ref[0, d, n, 0] = jnp.concatenate([uw[:, GDN_DV:], ch["qg"]], axis=0).astype(BF16)
            kd_ref[0, d, n, 0] = ch["kd"]
            a_ref[0, d, n, 0] = ch["amat"]
            eg_ref[0, d, n, 0] = ch["eg"]
        return carry

    lax.fori_loop(0, n_groups, group, 0)


GDN_SCAN_CHUNKS = 4


def _gdn_rec_kernel(uf, ub, wqf, wqb, kdf, kdb, af, ab, egf, egb, of_ref, ob_ref, s_scr):
    c = GDN_CHUNK
    ncs = GDN_SCAN_CHUNKS

    @pl.when(pl.program_id(1) == 0)
    def _():
        s_scr[...] = jnp.zeros_like(s_scr)

    streams = ((uf, wqf, kdf, af, egf, of_ref), (ub, wqb, kdb, ab, egb, ob_ref))
    ids = [(d, h) for d in range(2) for h in range(GDN_HEADS)]
    for t in range(ncs):
        cc = (t, ncs - 1 - t)
        sts = [s_scr[d, h] for d, h in ids]
        wqs = [_dot(streams[d][1][0, 0, cc[d], h], st.astype(BF16)) for (d, h), st in zip(ids, sts)]
        vbs = [(streams[d][0][0, 0, cc[d], h] - wq[:c]).astype(BF16) for (d, h), wq in zip(ids, wqs)]
        for (d, h), wq, vb in zip(ids, wqs, vbs):
            streams[d][5][0, cc[d] * c:(cc[d] + 1) * c, h * GDN_DV:(h + 1) * GDN_DV] = (
                wq[c:] + _dot(streams[d][3][0, 0, cc[d], h], vb))
        for (d, h), st, vb in zip(ids, sts, vbs):
            s_scr[d, h] = st * streams[d][4][0, 0, cc[d], h][0:1] + _dot_tn(streams[d][2][0, 0, cc[d], h], vb)


def _gdn_mixer(p, gates, conv_w, seq):
    b, t, _ = p.shape
    h, c = GDN_HEADS, GDN_CHUNK
    nc = t // c
    seq_c = seq // c
    ctx_c = nc - seq_c
    assert seq % c == 0 and t % (GDN_GROUP * c) == 0
    pool_blocks = (p.shape[2] - 4 * h * GDN_DK) // LANES
    gates = gates[..., :4 * h].reshape(b, t, 4, h)
    g_col = gates.transpose(0, 3, 1, 2)
    g_row = gates.reshape(b, nc, c, 4, h).transpose(0, 4, 1, 3, 2)
    col = lambda off: pl.BlockSpec((1, t, LANES), lambda b_, h_: (b_, 0, pool_blocks + off * h + h_))
    cw = lambda off: pl.BlockSpec((3, LANES), lambda b_, h_: (0, off * h + h_))
    pre_shapes = [((c, GDN_DV), F32), ((2 * c, GDN_DK), BF16), ((c, GDN_DK), BF16), ((c, c), BF16), ((8, LANES), F32)]
    pre = pl.pallas_call(
        functools.partial(_gdn_pre_kernel, seq=seq, t_all=t),
        grid=(b, h),
        in_specs=[col(0), col(1), col(2),
                  pl.BlockSpec((1, 1, t, 4), lambda b_, h_: (b_, h_, 0, 0)),
                  pl.BlockSpec((1, 1, nc, 4, c), lambda b_, h_: (b_, h_, 0, 0, 0)),
                  cw(0), cw(1), cw(2)],
        out_specs=[pl.BlockSpec((1, 2, nc, 1) + s, lambda b_, h_: (b_, 0, 0, h_, 0, 0)) for s, _ in pre_shapes],
        out_shape=[jax.ShapeDtypeStruct((b, 2, nc, h) + s, dt) for s, dt in pre_shapes],
        compiler_params=_cparams(("parallel", "parallel")),
        name="gated_delta_pre",
    )(p, p, p, g_col, g_row, conv_w, conv_w, conv_w)

    ncs = GDN_SCAN_CHUNKS
    assert seq_c % ncs == 0 and ctx_c % ncs == 0
    nblk, seq_b, ctx_b = nc // ncs, seq_c // ncs, ctx_c // ncs
    fwd_blk = lambda s: jnp.where(s < ctx_b, seq_b + s, s - ctx_b)
    bwd_blk = lambda s: nblk - 1 - s
    in_specs, args = [], []
    for arr, (s_, _) in zip(pre, pre_shapes):
        for d, blk in enumerate((fwd_blk, bwd_blk)):
            in_specs.append(pl.BlockSpec((1, 1, ncs, h) + s_, lambda b_, s, d=d, blk=blk: (b_, d, blk(s), 0, 0, 0)))
            args.append(arr)
    return pl.pallas_call(
        _gdn_rec_kernel,
        grid=(b, nblk),
        in_specs=in_specs,
        out_specs=[pl.BlockSpec((1, ncs * c, h * GDN_DV), lambda b_, s: (b_, fwd_blk(s), 0)),
                   pl.BlockSpec((1, ncs * c, h * GDN_DV), lambda b_, s: (b_, bwd_blk(s), 0))],
        out_shape=[jax.ShapeDtypeStruct((b, t, h * GDN_DV), F32)] * 2,
        scratch_shapes=[pltpu.VMEM((2, h, GDN_DK, GDN_DV), F32)],
        compiler_params=_cparams(("parallel", "arbitrary")),
        name="gated_delta_scan",
    )(*args)


ATTN_SUB = 256


def _attn_kernel(q_ref, qc_ref, k_ref, v_ref, lam_ref, g_ref, o_ref, *, seq, t_all, tq, lam_init):
    i = pl.program_id(2)
    hd = DIFF_HD
    n_ctx = t_all - seq
    lp = lam_ref[...]
    lam = (jnp.exp(jnp.sum(lp[0:1] * lp[1:2], axis=-1, keepdims=True))
           - jnp.exp(jnp.sum(lp[2:3] * lp[3:4], axis=-1, keepdims=True)) + lam_init)

    def attend(qs, k, v):
        ss = [_dot_nt(q[:, cpt * hd:(cpt + 1) * hd], k[:, cpt * hd:(cpt + 1) * hd]) for q in qs for cpt in range(2)]
        es = [jnp.exp2(s - jnp.max(s, axis=-1, keepdims=True)) for s in ss]
        inv = [1.0 / jnp.sum(e, axis=-1, keepdims=True) for e in es]
        pvs = [_dot(e.astype(BF16), v) for e in es]
        outs = []
        for r in range(len(qs)):
            o = pvs[2 * r] * inv[2 * r] - lam * (pvs[2 * r + 1] * inv[2 * r + 1])
            o = o * lax.rsqrt(jnp.mean(o * o, axis=-1, keepdims=True) + EPS) * g_ref[...] * (1.0 - lam_init)
            outs.append(o.astype(o_ref.dtype))
        return outs

    @pl.when(i * tq < seq)
    def _():
        outs = attend([q_ref[0, r:r + ATTN_SUB, :] for r in range(0, tq, ATTN_SUB)], k_ref[0], v_ref[0])
        for r, o in zip(range(0, tq, ATTN_SUB), outs):
            o_ref[0, r:r + ATTN_SUB, :] = o

    @pl.when(i * tq >= seq)
    def _():
        o_ref[0, 0:n_ctx, :] = attend([qc_ref[0]], k_ref[0, seq:t_all, :], v_ref[0, seq:t_all, :])[0]


def _diff_attention(qkv, lam_p, subln_g, seq, lam_init):
    b, t, _ = qkv.shape
    h, w = DIFF_HEADS, 2 * DIFF_HD
    n_ctx = t - seq
    tq = _pick(seq, (512, 256))
    assert tq % ATTN_SUB == 0 and 0 < n_ctx <= tq and seq % n_ctx == 0 and n_ctx % 16 == 0
    n_lat = seq // tq
    return pl.pallas_call(
        functools.partial(_attn_kernel, seq=seq, t_all=t, tq=tq, lam_init=lam_init),
        grid=(b, h, n_lat + 1),
        in_specs=[pl.BlockSpec((1, tq, w), lambda b_, h_, i: (b_, jnp.minimum(i, n_lat - 1), h_)),
                  pl.BlockSpec((1, n_ctx, w), lambda b_, h_, i: (b_, seq // n_ctx, h_)),
                  pl.BlockSpec((1, t, w), lambda b_, h_, i: (b_, 0, h + h_)),
                  pl.BlockSpec((1, t, w), lambda b_, h_, i: (b_, 0, 2 * h + h_)),
                  pl.BlockSpec((4, DIFF_HD), lambda b_, h_, i: (0, 0)),
                  pl.BlockSpec((1, w), lambda b_, h_, i: (0, 0))],
        out_specs=pl.BlockSpec((1, tq, w), lambda b_, h_, i: (b_, i, h_)),
        out_shape=jax.ShapeDtypeStruct((b, t, h * w), BF16),
        compiler_params=_cparams(("parallel", "parallel", "arbitrary")),
        name="diff_attention",
    )(qkv, qkv, qkv, qkv, lam_p, subln_g.reshape(1, w))


def _rope_tables(seq, t_all):
    rows = seq // GRID_W
    row = jnp.repeat(jnp.arange(rows), GRID_W).astype(F32)
    col = jnp.tile(jnp.arange(GRID_W), rows).astype(F32)
    n_freq = DIFF_HD // 4
    inv = ROPE_THETA ** (-jnp.arange(n_freq, dtype=F32) / n_freq)
    ang = jnp.concatenate([row[:, None] * inv, col[:, None] * inv], axis=-1)
    cos = jnp.concatenate([jnp.cos(ang), jnp.cos(ang)], axis=-1)
    sin = jnp.concatenate([-jnp.sin(ang), jnp.sin(ang)], axis=-1)
    pad = t_all - seq
    cos = jnp.concatenate([cos, jnp.ones((pad, DIFF_HD), F32)], axis=0)
    sin = jnp.concatenate([sin, jnp.zeros((pad, DIFF_HD), F32)], axis=0)
    return cos, sin


def kernel(x, c, ctx, c_ctx, w_mod, b_mod, ln1_g, ln1_b, ln2_g, ln2_b, ev_w_in, ev_conv, ev_a_log, ev_dt_bias, ev_pool_w, ev_pool_scale, ev_norm, ev_w_out, od_w_in, od_lam_q1, od_lam_k1, od_lam_q2, od_lam_k2, od_subln, od_w_out, ffn_w_up, ffn_conv, ffn_w_down):
    b, seq, d = x.shape
    t_all = seq + ctx.shape[1]
    depth = w_mod.shape[0]
    alpha = (2 * depth) ** 0.25
    pool_width = ev_pool_w.shape[1] * ev_pool_w.shape[2]
    gdn_main = pool_width + 4 * GDN_HEADS * GDN_DK
    diff_qw = DIFF_HEADS * 2 * DIFF_HD

    r_pad = -(-(b + 1) // 8) * 8
    cond = jnp.concatenate([c, c_ctx[None, :], jnp.zeros((r_pad - b - 1, d), F32)], axis=0)
    mods = _modulation(cond, w_mod, b_mod).reshape(depth, r_pad, 6, d)
    lat = mods[:, :b]
    cx = jnp.broadcast_to(mods[:, b:b + 1], lat.shape)
    zeros2 = jnp.zeros((depth, b, 2, d), F32)
    one_plus = jnp.array([0.0, 1.0, 0.0], F32)[:, None]
    mod1 = jnp.concatenate([lat[:, :, 0:3] + one_plus, cx[:, :, 0:3] + one_plus, zeros2], axis=2)
    mod2 = jnp.concatenate([lat[:, :, 3:6] + one_plus, cx[:, :, 3:6] + one_plus, zeros2], axis=2)

    cos, sin = _rope_tables(seq, t_all)
    xa = jnp.concatenate([x, ctx], axis=1)

    for l in range(depth):
        i = l // 2
        rows = seq if l == depth - 1 else t_all
        if l % 2 == 0:
            w_in = ev_w_in[i].astype(BF16)
            w_gate = jnp.pad(w_in[:, gdn_main:], ((0, 0), (0, LANES - (w_in.shape[1] - gdn_main))))
            n_gate = 2 * GDN_HEADS
            gate_prm = jnp.zeros((3, LANES), F32)
            gate_prm = gate_prm.at[0, n_gate:2 * n_gate].set(ev_a_log[i].reshape(-1))
            gate_prm = gate_prm.at[1, n_gate:2 * n_gate].set(ev_dt_bias[i].reshape(-1))
            gate_prm = gate_prm.at[2, :n_gate].set(1.0)
            p, gates = _mm1(xa, mod1[l], w_in[:, :gdn_main], seq, F32, gates=(w_gate, gate_prm))
            ya = _pool_mixer(p, ev_pool_w[i].astype(BF16), ev_pool_scale[i], seq)
            o_f, o_b = _gdn_mixer(p, gates, ev_conv[i], seq)
            w_out = ev_w_out[i].astype(BF16)
            z_blk = (gdn_main - o_f.shape[2]) // o_f.shape[2]
            xa = _mm2([ya], [w_out[:pool_width], w_out[pool_width:]], xa, mod1[l], ln1_g[l], ln1_b[l], seq, alpha,
                      rows, gdn=(o_f, o_b, p, z_blk, ev_norm[i]))
        else:
            lam_init = 0.8 - 0.6 * math.exp(-0.3 * l)
            qkv = _mm1(xa, mod1[l], od_w_in[i].astype(BF16), seq, BF16,
                       rope=(cos, sin, 2 * diff_qw, diff_qw, DIFF_HD ** -0.5 * math.log2(math.e)))
            lam_p = jnp.stack([od_lam_q1[i], od_lam_k1[i], od_lam_q2[i], od_lam_k2[i]]).astype(F32)
            y = _diff_attention(qkv, lam_p, od_subln[i], seq, lam_init)
            xa = _mm2([y], [od_w_out[i].astype(BF16)], xa, mod1[l], ln1_g[l], ln1_b[l], seq, alpha, rows)
        xa = _ffn(xa, mod2[l], ffn_w_up[l].astype(BF16), ffn_conv[l], ffn_w_down[l].astype(BF16),
                  ln2_g[l], ln2_b[l], seq, alpha, rows)
    return xa
```

```python
import functools
import math

import jax
import jax.numpy as jnp
from jax import lax
from jax.experimental import pallas as pl
from jax.experimental.pallas import tpu as pltpu

F32 = jnp.float32
BF16 = jnp.bfloat16

GRID_W = 64
POOL_GROUPS = 4
POOL_WINDOWS = (2, 4, 8, 16)
GDN_HEADS = 8
GDN_DK = 128
GDN_DV = 128
GDN_CHUNK = 64
DIFF_HEADS = 8
DIFF_HD = 128
ROPE_THETA = 10000.0
EPS = 1e-6

V7X_VMEM_BYTES = 64 * 1024 * 1024
VMEM_LIMIT = V7X_VMEM_BYTES - 8 * 1024 * 1024
LANES = 128
HALO = 16


def _cparams(sem):
    return pltpu.CompilerParams(dimension_semantics=sem, vmem_limit_bytes=VMEM_LIMIT)


def _sigmoid(x):
    return 1.0 / (1.0 + jnp.exp(-x))


def _silu(x):
    return x * _sigmoid(x)


def _dot(a, b):
    return jnp.dot(a, b, preferred_element_type=F32)


def _dot_nt(a, b):
    return lax.dot_general(a, b, (((1,), (1,)), ((), ())), preferred_element_type=F32)


def _dot_tn(a, b):
    return lax.dot_general(a, b, (((0,), (0,)), ((), ())), preferred_element_type=F32)


def _layer_norm_rows(r, g, b):
    mu = jnp.mean(r, axis=-1, keepdims=True)
    d = r - mu
    var = jnp.mean(d * d, axis=-1, keepdims=True)
    return d * lax.rsqrt(var + EPS) * g + b


def _pick(n, cands):
    for c in cands:
        if n % c == 0:
            return c
    raise ValueError(f"no tile in {cands} divides {n}")


def _mod_kernel(c_ref, w_ref, b_ref, o_ref):
    s = _silu(c_ref[...]).astype(BF16)
    o_ref[0] = _dot(s, w_ref[0].astype(BF16)) + b_ref[0]


def _modulation(cond, w_mod, b_mod):
    depth, d, n = w_mod.shape
    r = cond.shape[0]
    tn = _pick(n, (1024, 512, 256, 128))
    return pl.pallas_call(
        _mod_kernel,
        grid=(depth, n // tn),
        in_specs=[pl.BlockSpec((r, d), lambda l, j: (0, 0)),
                  pl.BlockSpec((1, d, tn), lambda l, j: (l, 0, j)),
                  pl.BlockSpec((1, 1, tn), lambda l, j: (l, 0, j))],
        out_specs=pl.BlockSpec((1, r, tn), lambda l, j: (l, 0, j)),
        out_shape=jax.ShapeDtypeStruct((depth, r, n), F32),
        compiler_params=_cparams(("parallel", "parallel")),
        name="modulation",
    )(cond, w_mod, b_mod.reshape(depth, 1, n))


def _per_segment(i, tm, seq, fn):
    i_mix, split = divmod(seq, tm)
    assert split % 16 == 0

    @pl.when(i < i_mix)
    def _():
        fn(slice(0, tm), 0)

    @pl.when(i > i_mix)
    def _():
        fn(slice(0, tm), 1)

    @pl.when(i == i_mix)
    def _():
        if split:
            fn(slice(0, split), 0)
        fn(slice(split, tm), 1)


def _mm1_kernel(*refs, seq, tm, rope_tiles, q_tiles, q_scale, gates):
    if rope_tiles:
        x_ref, mod_ref, w_ref, cos_ref, sin_ref, o_ref, u_scr = refs
    elif gates:
        x_ref, mod_ref, w_ref, wg_ref, ga_ref, o_ref, og_ref, u_scr = refs
    else:
        x_ref, mod_ref, w_ref, o_ref, u_scr = refs
    i = pl.program_id(1)
    j = pl.program_id(2)

    @pl.when(j == 0)
    def _():
        def modulate(rs, k):
            m = mod_ref[0]
            u_scr[rs] = (x_ref[0, rs, :] * m[3 * k + 1:3 * k + 2] + m[3 * k:3 * k + 1]).astype(BF16)

        _per_segment(i, tm, seq, modulate)
        if gates:
            raw = _dot(u_scr[...], wg_ref[...])
            ga = ga_ref[...]
            xg = raw + ga[1:2]
            softplus = jnp.maximum(xg, 0.0) + jnp.log(1.0 + jnp.exp(-jnp.abs(xg)))
            og_ref[0] = jnp.where(ga[2:3] > 0.5, _sigmoid(raw), -jnp.exp(ga[0:1]) * softplus)

    acc = _dot(u_scr[...], w_ref[...])
    if not rope_tiles:
        o_ref[0] = acc.astype(o_ref.dtype)
        return

    @pl.when(j < rope_tiles)
    def _():
        sc = jnp.where(j < q_tiles, q_scale, 1.0).astype(F32)
        cos = cos_ref[...] * sc
        sin = sin_ref[...] * sc
        for g in range(acc.shape[1] // LANES):
            blk = acc[:, g * LANES:(g + 1) * LANES]
            rot = pltpu.roll(blk, LANES // 2, axis=1)
            o_ref[0, :, g * LANES:(g + 1) * LANES] = (blk * cos + rot * sin).astype(o_ref.dtype)

    @pl.when(j >= rope_tiles)
    def _():
        o_ref[0] = acc.astype(o_ref.dtype)


def _mm1(xa, mod, w, seq, out_dtype, rope=None, gates=None):
    b, t, d = xa.shape
    n = w.shape[1]
    tm = _pick(t, (1152, 768, 512, 256, 128))
    tn = _pick(n, (1024, 512, 256, 128))
    in_specs = [pl.BlockSpec((1, tm, d), lambda b_, i, j: (b_, i, 0)),
                pl.BlockSpec((1, 8, d), lambda b_, i, j: (b_, 0, 0)),
                pl.BlockSpec((d, tn), lambda b_, i, j: (0, j))]
    args = [xa, mod, w]
    kw = dict(seq=seq, tm=tm, rope_tiles=0, q_tiles=0, q_scale=1.0, gates=gates is not None)
    out_specs = pl.BlockSpec((1, tm, tn), lambda b_, i, j: (b_, i, j))
    out_shape = jax.ShapeDtypeStruct((b, t, n), out_dtype)
    if gates is not None:
        in_specs += [pl.BlockSpec((d, LANES), lambda b_, i, j: (0, 0)), pl.BlockSpec((3, LANES), lambda b_, i, j: (0, 0))]
        args += list(gates)
        out_specs = [out_specs, pl.BlockSpec((1, tm, LANES), lambda b_, i, j: (b_, i, 0))]
        out_shape = [out_shape, jax.ShapeDtypeStruct((b, t, LANES), F32)]
    if rope is not None:
        cos, sin, rope_cols, q_cols, q_scale = rope
        assert rope_cols % tn == 0 and q_cols % tn == 0
        in_specs += [pl.BlockSpec((tm, LANES), lambda b_, i, j: (i, 0)),
                     pl.BlockSpec((tm, LANES), lambda b_, i, j: (i, 0))]
        args += [cos, sin]
        kw.update(rope_tiles=rope_cols // tn, q_tiles=q_cols // tn, q_scale=q_scale)
    return pl.pallas_call(
        functools.partial(_mm1_kernel, **kw),
        grid=(b, t // tm, n // tn),
        in_specs=in_specs,
        out_specs=out_specs,
        out_shape=out_shape,
        scratch_shapes=[pltpu.VMEM((tm, d), BF16)],
        compiler_params=_cparams(("parallel", "parallel", "arbitrary")),
        name="mod_in_proj",
    )(*args)


def _mm2_kernel(*refs, gdn, seq, tm, alpha):
    if gdn:
        ya_ref, of_ref, ob_ref, z_ref, ng_ref, wa_ref, wb_ref, x_ref, mod_ref, g_ref, b_ref, o_ref, yb_scr = refs
        for h in range(GDN_HEADS):
            sl = slice(h * GDN_DV, (h + 1) * GDN_DV)
            o = of_ref[0, :, sl] + ob_ref[0, :, sl]
            o = o * lax.rsqrt(jnp.mean(o * o, axis=-1, keepdims=True) + EPS) * ng_ref[...]
            yb_scr[:, sl] = (o * _silu(z_ref[0, :, sl])).astype(BF16)
        acc = _dot(ya_ref[0], wa_ref[...]) + _dot(yb_scr[...], wb_ref[...])
    else:
        y_ref, w_ref, x_ref, mod_ref, g_ref, b_ref, o_ref = refs
        acc = _dot(y_ref[0], w_ref[...])

    def finish(rs, k):
        r = alpha * x_ref[0, rs, :] + mod_ref[0][3 * k + 2:3 * k + 3] * acc[rs]
        o_ref[0, rs, :] = _layer_norm_rows(r, g_ref[...], b_ref[...])

    _per_segment(pl.program_id(1), tm, seq, finish)


def _mm2(ys, ws, xa, mod, ln_g, ln_b, seq, alpha, out_rows, gdn=None):
    b, t, d = xa.shape
    tm = _pick(out_rows, ((512,) if gdn is None else ()) + (384, 256, 128))
    row_blk = lambda width, cb=0: pl.BlockSpec((1, tm, width), lambda b_, i: (b_, i, cb))
    full = lambda a: pl.BlockSpec(a.shape, lambda b_, i: (0,) * a.ndim)
    in_specs = [row_blk(ys[0].shape[2])]
    args = [ys[0]]
    scratch = []
    if gdn is not None:
        o_f, o_b, p, z_blk, norm_g = gdn
        zw = o_f.shape[2]
        in_specs += [row_blk(zw), row_blk(zw), row_blk(zw, z_blk), pl.BlockSpec((1, GDN_DV), lambda b_, i: (0, 0))]
        args += [o_f, o_b, p, norm_g.reshape(1, GDN_DV)]
        scratch = [pltpu.VMEM((tm, zw), BF16)]
    in_specs += [full(w) for w in ws]
    in_specs += [row_blk(d), pl.BlockSpec((1, 8, d), lambda b_, i: (b_, 0, 0)),
                 pl.BlockSpec((1, d), lambda b_, i: (0, 0)), pl.BlockSpec((1, d), lambda b_, i: (0, 0))]
    return pl.pallas_call(
        functools.partial(_mm2_kernel, gdn=gdn is not None, seq=seq, tm=tm, alpha=alpha),
        grid=(b, out_rows // tm),
        in_specs=in_specs,
        out_specs=row_blk(d),
        out_shape=jax.ShapeDtypeStruct((b, out_rows, d), F32),
        scratch_shapes=scratch,
        compiler_params=_cparams(("parallel", "parallel")),
        name="out_proj_ln",
    )(*args, *ws, xa, mod, ln_g.reshape(1, d), ln_b.reshape(1, d))


def _ffn_kernel(x_ref, xp_ref, xn_ref, mod_ref, wg_ref, wu_ref, cw_ref, wd_ref, g_ref, b_ref, o_ref,
                u_scr, acc_scr, *, seq, t_all, tm, alpha):
    i = pl.program_id(1)
    j = pl.program_id(2)
    nj = pl.num_programs(2)
    row0 = i * tm

    @pl.when(j == 0)
    def _():
        m = mod_ref[0]

        def modulate(rs, k):
            u_scr[HALO + rs.start:HALO + rs.stop] = (
                x_ref[0, rs, :] * m[3 * k + 1:3 * k + 2] + m[3 * k:3 * k + 1]).astype(BF16)

        def halo(x, in_latent):
            shift = jnp.where(in_latent, m[0:1], m[3:4])
            scale1p = jnp.where(in_latent, m[1:2], m[4:5])
            return (x * scale1p + shift).astype(BF16)

        u_scr[0:HALO] = halo(xp_ref[0], row0 < seq)
        _per_segment(i, tm, seq, modulate)
        u_scr[HALO + tm:2 * HALO + tm] = halo(xn_ref[0], row0 + tm - 1 < seq)
        acc_scr[...] = jnp.zeros_like(acc_scr)

    gate_ext = _dot(u_scr[...], wg_ref[...])
    up = _dot(u_scr[HALO:HALO + tm], wu_ref[...])
    gm = gate_ext[HALO:HALO + tm]
    lrow = lax.broadcasted_iota(jnp.int32, (tm, 1), 0)
    grow = row0 + lrow
    prev = jnp.where(lrow == 0, gate_ext[HALO - 1:HALO], pltpu.roll(gm, 1, axis=0))
    nxt = jnp.where(lrow == tm - 1, gate_ext[HALO + tm:HALO + tm + 1], pltpu.roll(gm, tm - 1, axis=0))
    prev = jnp.where((grow == 0) | (grow == seq), 0.0, prev)
    nxt = jnp.where((grow == seq - 1) | (grow == t_all - 1), 0.0, nxt)
    cw = cw_ref[...]
    conv = cw[0:1] * prev + cw[1:2] * gm + cw[2:3] * nxt
    h = (_silu(conv) * up).astype(BF16)
    acc_scr[...] += _dot(h, wd_ref[...])

    @pl.when(j == nj - 1)
    def _():
        def finish(rs, k):
            r = alpha * x_ref[0, rs, :] + mod_ref[0][3 * k + 2:3 * k + 3] * acc_scr[rs]
            o_ref[0, rs, :] = _layer_norm_rows(r, g_ref[...], b_ref[...])

        _per_segment(i, tm, seq, finish)


def _ffn(xa, mod, w_up, conv_w, w_down, ln_g, ln_b, seq, alpha, out_rows):
    b, t, d = xa.shape
    dff = w_down.shape[0]
    tm = _pick(out_rows, (768, 512, 256, 128))
    tf = _pick(dff, (512, 256, 128))
    nj = dff // tf
    hb = tm // HALO
    last_hb = t // HALO - 1
    return pl.pallas_call(
        functools.partial(_ffn_kernel, seq=seq, t_all=t, tm=tm, alpha=alpha),
        grid=(b, pl.cdiv(out_rows, tm), nj),
        in_specs=[pl.BlockSpec((1, tm, d), lambda b_, i, j: (b_, i, 0)),
                  pl.BlockSpec((1, HALO, d), lambda b_, i, j: (b_, jnp.maximum(i * hb - 1, 0), 0)),
                  pl.BlockSpec((1, HALO, d), lambda b_, i, j: (b_, jnp.minimum((i + 1) * hb, last_hb), 0)),
                  pl.BlockSpec((1, 8, d), lambda b_, i, j: (b_, 0, 0)),
                  pl.BlockSpec((d, tf), lambda b_, i, j: (0, j)),
                  pl.BlockSpec((d, tf), lambda b_, i, j: (0, nj + j)),
                  pl.BlockSpec((3, tf), lambda b_, i, j: (0, j)),
                  pl.BlockSpec((tf, d), lambda b_, i, j: (j, 0)),
                  pl.BlockSpec((1, d), lambda b_, i, j: (0, 0)),
                  pl.BlockSpec((1, d), lambda b_, i, j: (0, 0))],
        out_specs=pl.BlockSpec((1, tm, d), lambda b_, i, j: (b_, i, 0)),
        out_shape=jax.ShapeDtypeStruct((b, out_rows, d), F32),
        scratch_shapes=[pltpu.VMEM((tm + 2 * HALO, d), BF16), pltpu.VMEM((tm, d), F32)],
        compiler_params=_cparams(("parallel", "parallel", "arbitrary")),
        name="conv_glu_ffn",
    )(xa, xa, xa, mod, w_up, w_up, conv_w, w_down, ln_g.reshape(1, d), ln_b.reshape(1, d))


def _pool_kernel(a_ref, w_ref, s_ref, o_ref, *, seq, t_all):
    g = pl.program_id(1)
    a = a_ref[0].astype(F32)
    t = lax.broadcasted_iota(jnp.int32, (t_all, 1), 0)
    pos = jnp.where(t < seq, t, t - seq)
    rem = jnp.where(t < seq, seq - 1 - t, t_all - 1 - t)
    for gi, win in enumerate(POOL_WINDOWS):
        @pl.when(g == gi)
        def _(win=win):
            before, after = win // 2, win - 1 - win // 2
            tot = a
            for dlt in range(1, before + 1):
                tot = tot + jnp.where(pos >= dlt, pltpu.roll(a, dlt, axis=0), 0.0)
            for dlt in range(1, after + 1):
                tot = tot + jnp.where(rem >= dlt, pltpu.roll(a, t_all - dlt, axis=0), 0.0)
            cnt = (1 + jnp.minimum(pos, before) + jnp.minimum(rem, after)).astype(F32)
            pooled = (tot * (1.0 / cnt) - a).astype(BF16)
            o_ref[0] = (_dot(pooled, w_ref[0]) * s_ref[0]).astype(o_ref.dtype)


def _pool_mixer(p, pool_w, pool_scale, seq):
    b, t, _ = p.shape
    g, gw, _ = pool_w.shape
    return pl.pallas_call(
        functools.partial(_pool_kernel, seq=seq, t_all=t),
        grid=(b, g),
        in_specs=[pl.BlockSpec((1, t, gw), lambda b_, g_: (b_, 0, g_)),
                  pl.BlockSpec((1, gw, gw), lambda b_, g_: (g_, 0, 0)),
                  pl.BlockSpec((1, 1, gw), lambda b_, g_: (g_, 0, 0))],
        out_specs=pl.BlockSpec((1, t, gw), lambda b_, g_: (b_, 0, g_)),
        out_shape=jax.ShapeDtypeStruct((b, t, g * gw), BF16),
        compiler_params=_cparams(("parallel", "parallel")),
        name="pool_mixer",
    )(p, pool_w, pool_scale.reshape(g, 1, gw))


GDN_GROUP = 12


def _gdn_pre_kernel(q_ref, k_ref, v_ref, gc_ref, gr_ref, cwq_ref, cwk_ref, cwv_ref,
                    u_ref, wq_ref, kd_ref, a_ref, eg_ref, *, seq, t_all):
    c = GDN_CHUNK
    gsz = GDN_GROUP
    rows = gsz * c
    n_groups = t_all // rows
    ri = lax.broadcasted_iota(jnp.int32, (c, c), 0)
    ci = lax.broadcasted_iota(jnp.int32, (c, c), 1)
    lower = ci <= ri
    upper = ci >= ri
    eye = (ci == ri).astype(F32)
    row = lax.broadcasted_iota(jnp.int32, (rows, 1), 0)

    def group(gi, carry):
        r0 = pl.multiple_of(gi * rows, rows)
        n0 = gi * gsz
        grow_idx = r0 + row
        no_prev = (grow_idx == 0) | (grow_idx == seq)
        no_next = (grow_idx == seq - 1) | (grow_idx == t_all - 1)
        pidx = jnp.maximum(r0 - 1, 0)
        nidx = jnp.minimum(r0 + rows, t_all - 1)

        def conv_silu(ref, cw_ref):
            xm = ref[0, pl.ds(r0, rows), :]
            xp = jnp.where(row == 0, ref[0, pl.ds(pidx, 1), :], pltpu.roll(xm, 1, axis=0))
            xn = jnp.where(row == rows - 1, ref[0, pl.ds(nidx, 1), :], pltpu.roll(xm, rows - 1, axis=0))
            xp = jnp.where(no_prev, 0.0, xp)
            xn = jnp.where(no_next, 0.0, xn)
            cw = cw_ref[...]
            return _silu(cw[0:1] * xp + cw[1:2] * xm + cw[2:3] * xn)

        def l2n(x):
            return x * lax.rsqrt(jnp.sum(x * x, axis=-1, keepdims=True) + EPS)

        q_all = l2n(conv_silu(q_ref, cwq_ref)) * (GDN_DK ** -0.5)
        k_all = l2n(conv_silu(k_ref, cwk_ref))
        v_all = conv_silu(v_ref, cwv_ref)
        gcol_all = gc_ref[0, 0, pl.ds(r0, rows), :]
        chunk = lambda x, j: x[j * c:(j + 1) * c]
        qs = [chunk(q_all, j) for j in range(gsz)]
        ks = [chunk(k_all, j) for j in range(gsz)]
        vs = [chunk(v_all, j) for j in range(gsz)]
        kbs = [k.astype(BF16) for k in ks]
        qks = [_dot_nt(q.astype(BF16), kb) for q, kb in zip(qs, kbs)]
        kks = [_dot_nt(kb, kb) for kb in kbs]
        chains = []
        for j in range(gsz):
            gcol = chunk(gcol_all, j)
            grow = gr_ref[0, 0, n0 + j]
            for d in range(2):
                beta = gcol[:, d:d + 1]
                g_c = gcol[:, 2 + d:3 + d]
                g_r = grow[2 + d:3 + d, :]
                incl = lower if d == 0 else upper
                strict = (ci < ri) if d == 0 else (ci > ri)
                inclt = upper if d == 0 else lower
                gcc = jnp.sum(jnp.where(incl, g_r, 0.0), axis=1, keepdims=True)
                gcr = jnp.sum(jnp.where(inclt, g_c, 0.0), axis=0, keepdims=True)
                decay = jnp.exp(jnp.where(incl, gcc - gcr, -jnp.inf))
                g_last = gcc[c - 1:c] if d == 0 else gcc[0:1]
                lmat = jnp.where(strict, kks[j] * beta * decay, 0.0)
                egc = jnp.exp(gcc)
                rhs = jnp.concatenate([vs[j] * beta, ks[j] * (beta * egc)], axis=1).astype(BF16)
                chains.append(dict(j=j, d=d, lmat=lmat, rhs=rhs, qg=qs[j] * egc,
                                   kd=(ks[j] * jnp.exp(g_last - gcc)).astype(BF16),
                                   amat=(qks[j] * decay).astype(BF16),
                                   eg=jnp.broadcast_to(jnp.exp(g_last), (8, LANES))))
        tmats = [eye - ch["lmat"] for ch in chains]
        pws = [ch["lmat"] for ch in chains]
        for _ in range(int(math.log2(c)) - 1):
            pws = [_dot(pw.astype(BF16), pw.astype(BF16)) for pw in pws]
            tmats = [tm + _dot(tm.astype(BF16), pw.astype(BF16)) for tm, pw in zip(tmats, pws)]
        uws = [_dot(tm.astype(BF16), ch["rhs"]) for tm, ch in zip(tmats, chains)]
        for uw, ch in zip(uws, chains):
            d, n = ch["d"], n0 + ch["j"]
            u_ref[0, d, n, 0] = uw[:, :GDN_DV]
            wq_ref[0, d, n, 0] = jnp.concatenate([uw[:, GDN_DV:], ch["qg"]], axis=0).astype(BF16)
            kd_ref[0, d, n, 0] = ch["kd"]
            a_ref[0, d, n, 0] = ch["amat"]
            eg_ref[0, d, n, 0] = ch["eg"]
        return carry

    lax.fori_loop(0, n_groups, group, 0)


GDN_SCAN_CHUNKS = 4


def _gdn_rec_kernel(uf, ub, wqf, wqb, kdf, kdb, af, ab, egf, egb, of_ref, ob_ref, s_scr):
    c = GDN_CHUNK
    ncs = GDN_SCAN_CHUNKS

    @pl.when(pl.program_id(1) == 0)
    def _():
        s_scr[...] = jnp.zeros_like(s_scr)

    streams = ((uf, wqf, kdf, af, egf, of_ref), (ub, wqb, kdb, ab, egb, ob_ref))
    ids = [(d, h) for d in range(2) for h in range(GDN_HEADS)]
    for t in range(ncs):
        cc = (t, ncs - 1 - t)
        sts = [s_scr[d, h] for d, h in ids]
        wqs = [_dot(streams[d][1][0, 0, cc[d], h], st.astype(BF16)) for (d, h), st in zip(ids, sts)]
        vbs = [(streams[d][0][0, 0, cc[d], h] - wq[:c]).astype(BF16) for (d, h), wq in zip(ids, wqs)]
        for (d, h), wq, vb in zip(ids, wqs, vbs):
            streams[d][5][0, cc[d] * c:(cc[d] + 1) * c, h * GDN_DV:(h + 1) * GDN_DV] = (
                wq[c:] + _dot(streams[d][3][0, 0, cc[d], h], vb))
        for (d, h), st, vb in zip(ids, sts, vbs):
            s_scr[d, h] = st * streams[d][4][0, 0, cc[d], h][0:1] + _dot_tn(streams[d][2][0, 0, cc[d], h], vb)


def _gdn_mixer(p, gates, conv_w, seq):
    b, t, _ = p.shape
    h, c = GDN_HEADS, GDN_CHUNK
    nc = t // c
    seq_c = seq // c
    ctx_c = nc - seq_c
    assert seq % c == 0 and t % (GDN_GROUP * c) == 0
    pool_blocks = (p.shape[2] - 4 * h * GDN_DK) // LANES
    gates = gates[..., :4 * h].reshape(b, t, 4, h)
    g_col = gates.transpose(0, 3, 1, 2)
    g_row = gates.reshape(b, nc, c, 4, h).transpose(0, 4, 1, 3, 2)
    col = lambda off: pl.BlockSpec((1, t, LANES), lambda b_, h_: (b_, 0, pool_blocks + off * h + h_))
    cw = lambda off: pl.BlockSpec((3, LANES), lambda b_, h_: (0, off * h + h_))
    pre_shapes = [((c, GDN_DV), F32), ((2 * c, GDN_DK), BF16), ((c, GDN_DK), BF16), ((c, c), BF16), ((8, LANES), F32)]
    pre = pl.pallas_call(
        functools.partial(_gdn_pre_kernel, seq=seq, t_all=t),
        grid=(b, h),
        in_specs=[col(0), col(1), col(2),
                  pl.BlockSpec((1, 1, t, 4), lambda b_, h_: (b_, h_, 0, 0)),
                  pl.BlockSpec((1, 1, nc, 4, c), lambda b_, h_: (b_, h_, 0, 0, 0)),
                  cw(0), cw(1), cw(2)],
        out_specs=[pl.BlockSpec((1, 2, nc, 1) + s, lambda b_, h_: (b_, 0, 0, h_, 0, 0)) for s, _ in pre_shapes],
        out_shape=[jax.ShapeDtypeStruct((b, 2, nc, h) + s, dt) for s, dt in pre_shapes],
        compiler_params=_cparams(("parallel", "parallel")),
        name="gated_delta_pre",
    )(p, p, p, g_col, g_row, conv_w, conv_w, conv_w)

    ncs = GDN_SCAN_CHUNKS
    assert seq_c % ncs == 0 and ctx_c % ncs == 0
    nblk, seq_b, ctx_b = nc // ncs, seq_c // ncs, ctx_c // ncs
    fwd_blk = lambda s: jnp.where(s < ctx_b, seq_b + s, s - ctx_b)
    bwd_blk = lambda s: nblk - 1 - s
    in_specs, args = [], []
    for arr, (s_, _) in zip(pre, pre_shapes):
        for d, blk in enumerate((fwd_blk, bwd_blk)):
            in_specs.append(pl.BlockSpec((1, 1, ncs, h) + s_, lambda b_, s, d=d, blk=blk: (b_, d, blk(s), 0, 0, 0)))
            args.append(arr)
    return pl.pallas_call(
        _gdn_rec_kernel,
        grid=(b, nblk),
        in_specs=in_specs,
        out_specs=[pl.BlockSpec((1, ncs * c, h * GDN_DV), lambda b_, s: (b_, fwd_blk(s), 0)),
                   pl.BlockSpec((1, ncs * c, h * GDN_DV), lambda b_, s: (b_, bwd_blk(s), 0))],
        out_shape=[jax.ShapeDtypeStruct((b, t, h * GDN_DV), F32)] * 2,
        scratch_shapes=[pltpu.VMEM((2, h, GDN_DK, GDN_DV), F32)],
        compiler_params=_cparams(("parallel", "arbitrary")),
        name="gated_delta_scan",
    )(*args)


ATTN_SUB = 256


def _attn_kernel(q_ref, qc_ref, k_ref, v_ref, lam_ref, g_ref, o_ref, *, seq, t_all, tq, lam_init):
    i = pl.program_id(2)
    hd = DIFF_HD
    n_ctx = t_all - seq
    lp = lam_ref[...]
    lam = (jnp.exp(jnp.sum(lp[0:1] * lp[1:2], axis=-1, keepdims=True))
           - jnp.exp(jnp.sum(lp[2:3] * lp[3:4], axis=-1, keepdims=True)) + lam_init)

    def attend(qs, k, v):
        ss = [_dot_nt(q[:, cpt * hd:(cpt + 1) * hd], k[:, cpt * hd:(cpt + 1) * hd]) for q in qs for cpt in range(2)]
        es = [jnp.exp2(s - jnp.max(s, axis=-1, keepdims=True)) for s in ss]
        inv = [1.0 / jnp.sum(e, axis=-1, keepdims=True) for e in es]
        pvs = [_dot(e.astype(BF16), v) for e in es]
        outs = []
        for r in range(len(qs)):
            o = pvs[2 * r] * inv[2 * r] - lam * (pvs[2 * r + 1] * inv[2 * r + 1])
            o = o * lax.rsqrt(jnp.mean(o * o, axis=-1, keepdims=True) + EPS) * g_ref[...] * (1.0 - lam_init)
            outs.append(o.astype(o_ref.dtype))
        return outs

    @pl.when(i * tq < seq)
    def _():
        outs = attend([q_ref[0, r:r + ATTN_SUB, :] for r in range(0, tq, ATTN_SUB)], k_ref[0], v_ref[0])
        for r, o in zip(range(0, tq, ATTN_SUB), outs):
            o_ref[0, r:r + ATTN_SUB, :] = o

    @pl.when(i * tq >= seq)
    def _():
        o_ref[0, 0:n_ctx, :] = attend([qc_ref[0]], k_ref[0, seq:t_all, :], v_ref[0, seq:t_all, :])[0]


def _diff_attention(qkv, lam_p, subln_g, seq, lam_init):
    b, t, _ = qkv.shape
    h, w = DIFF_HEADS, 2 * DIFF_HD
    n_ctx = t - seq
    tq = _pick(seq, (512, 256))
    assert tq % ATTN_SUB == 0 and 0 < n_ctx <= tq and seq % n_ctx == 0 and n_ctx % 16 == 0
    n_lat = seq // tq
    return pl.pallas_call(
        functools.partial(_attn_kernel, seq=seq, t_all=t, tq=tq, lam_init=lam_init),
        grid=(b, h, n_lat + 1),
        in_specs=[pl.BlockSpec((1, tq, w), lambda b_, h_, i: (b_, jnp.minimum(i, n_lat - 1), h_)),
                  pl.BlockSpec((1, n_ctx, w), lambda b_, h_, i: (b_, seq // n_ctx, h_)),
                  pl.BlockSpec((1, t, w), lambda b_, h_, i: (b_, 0, h + h_)),
                  pl.BlockSpec((1, t, w), lambda b_, h_, i: (b_, 0, 2 * h + h_)),
                  pl.BlockSpec((4, DIFF_HD), lambda b_, h_, i: (0, 0)),
                  pl.BlockSpec((1, w), lambda b_, h_, i: (0, 0))],
        out_specs=pl.BlockSpec((1, tq, w), lambda b_, h_, i: (b_, i, h_)),
        out_shape=jax.ShapeDtypeStruct((b, t, h * w), BF16),
        compiler_params=_cparams(("parallel", "parallel", "arbitrary")),
        name="diff_attention",
    )(qkv, qkv, qkv, qkv, lam_p, subln_g.reshape(1, w))


def _rope_tables(seq, t_all):
    rows = seq // GRID_W
    row = jnp.repeat(jnp.arange(rows), GRID_W).astype(F32)
    col = jnp.tile(jnp.arange(GRID_W), rows).astype(F32)
    n_freq = DIFF_HD // 4
    inv = ROPE_THETA ** (-jnp.arange(n_freq, dtype=F32) / n_freq)
    ang = jnp.concatenate([row[:, None] * inv, col[:, None] * inv], axis=-1)
    cos = jnp.concatenate([jnp.cos(ang), jnp.cos(ang)], axis=-1)
    sin = jnp.concatenate([-jnp.sin(ang), jnp.sin(ang)], axis=-1)
    pad = t_all - seq
    cos = jnp.concatenate([cos, jnp.ones((pad, DIFF_HD), F32)], axis=0)
    sin = jnp.concatenate([sin, jnp.zeros((pad, DIFF_HD), F32)], axis=0)
    return cos, sin


def kernel(x, c, ctx, c_ctx, w_mod, b_mod, ln1_g, ln1_b, ln2_g, ln2_b, ev_w_in, ev_conv, ev_a_log, ev_dt_bias, ev_pool_w, ev_pool_scale, ev_norm, ev_w_out, od_w_in, od_lam_q1, od_lam_k1, od_lam_q2, od_lam_k2, od_subln, od_w_out, ffn_w_up, ffn_conv, ffn_w_down):
    b, seq, d = x.shape
    t_all = seq + ctx.shape[1]
    depth = w_mod.shape[0]
    alpha = (2 * depth) ** 0.25
    pool_width = ev_pool_w.shape[1] * ev_pool_w.shape[2]
    gdn_main = pool_width + 4 * GDN_HEADS * GDN_DK
    diff_qw = DIFF_HEADS * 2 * DIFF_HD

    r_pad = -(-(b + 1) // 8) * 8
    cond = jnp.concatenate([c, c_ctx[None, :], jnp.zeros((r_pad - b - 1, d), F32)], axis=0)
    mods = _modulation(cond, w_mod, b_mod).reshape(depth, r_pad, 6, d)
    lat = mods[:, :b]
    cx = jnp.broadcast_to(mods[:, b:b + 1], lat.shape)
    zeros2 = jnp.zeros((depth, b, 2, d), F32)
    one_plus = jnp.array([0.0, 1.0, 0.0], F32)[:, None]
    mod1 = jnp.concatenate([lat[:, :, 0:3] + one_plus, cx[:, :, 0:3] + one_plus, zeros2], axis=2)
    mod2 = jnp.concatenate([lat[:, :, 3:6] + one_plus, cx[:, :, 3:6] + one_plus, zeros2], axis=2)

    cos, sin = _rope_tables(seq, t_all)
    xa = jnp.concatenate([x, ctx], axis=1)

    for l in range(depth):
        i = l // 2
        rows = seq if l == depth - 1 else t_all
        if l % 2 == 0:
            w_in = ev_w_in[i].astype(BF16)
            w_gate = jnp.pad(w_in[:, gdn_main:], ((0, 0), (0, LANES - (w_in.shape[1] - gdn_main))))
            n_gate = 2 * GDN_HEADS
            gate_prm = jnp.zeros((3, LANES), F32)
            gate_prm = gate_prm.at[0, n_gate:2 * n_gate].set(ev_a_log[i].reshape(-1))
            gate_prm = gate_prm.at[1, n_gate:2 * n_gate].set(ev_dt_bias[i].reshape(-1))
            gate_prm = gate_prm.at[2, :n_gate].set(1.0)
            p, gates = _mm1(xa, mod1[l], w_in[:, :gdn_main], seq, F32, gates=(w_gate, gate_prm))
            ya = _pool_mixer(p, ev_pool_w[i].astype(BF16), ev_pool_scale[i], seq)
            o_f, o_b = _gdn_mixer(p, gates, ev_conv[i], seq)
            w_out = ev_w_out[i].astype(BF16)
            z_blk = (gdn_main - o_f.shape[2]) // o_f.shape[2]
            xa = _mm2([ya], [w_out[:pool_width], w_out[pool_width:]], xa, mod1[l], ln1_g[l], ln1_b[l], seq, alpha,
                      rows, gdn=(o_f, o_b, p, z_blk, ev_norm[i]))
        else:
            lam_init = 0.8 - 0.6 * math.exp(-0.3 * l)
            qkv = _mm1(xa, mod1[l], od_w_in[i].astype(BF16), seq, BF16,
                       rope=(cos, sin, 2 * diff_qw, diff_qw, DIFF_HD ** -0.5 * math.log2(math.e)))
            lam_p = jnp.stack([od_lam_q1[i], od_lam_k1[i], od_lam_q2[i], od_lam_k2[i]]).astype(F32)
            y = _diff_attention(qkv, lam_p, od_subln[i], seq, lam_init)
            xa = _mm2([y], [od_w_out[i].astype(BF16)], xa, mod1[l], ln1_g[l], ln1_b[l], seq, alpha, rows)
        xa = _ffn(xa, mod2[l], ffn_w_up[l].astype(BF16), ffn_conv[l], ffn_w_down[l].astype(BF16),
                  ln2_g[l], ln2_b[l], seq, alpha, rows)
    return xa
```
